```python
import jax
import jax.numpy as jnp
from jax import lax
import numpy as np


D_MODEL = 4096
BATCH = 1
SEQ = 16384
DEPTH = 4

GRID_W = 64
CTX_LEN = 256
N_MIXERS = 3
Q_BLOCK = 128
ROPE_THETA = 10000.0
NORM_EPS = 1e-6
NEG_INF = -1e30
N_MOD = 6

A_HEAD_DIM = 128
A_Q_HEADS = D_MODEL // A_HEAD_DIM
A_KV_HEADS = A_Q_HEADS // 4

B_HEAD_DIM = 64
B_Q_HEADS = D_MODEL // B_HEAD_DIM
B_KV_HEADS = B_Q_HEADS // 8
WINDOW = 128

FNET_GROUPS = 8

PEER_HEADS = 8
PEER_N_KEYS = 96
PEER_EXPERTS = PEER_N_KEYS * PEER_N_KEYS
PEER_TOPK = 16
PEER_QUERY_DIM = 256
PEER_HALF = PEER_QUERY_DIM // 2
PEER_BLOCK = 64

N_A_LAYERS = (DEPTH + N_MIXERS - 1) // N_MIXERS
N_B_LAYERS = (DEPTH + N_MIXERS - 2) // N_MIXERS
N_C_LAYERS = (DEPTH + N_MIXERS - 3) // N_MIXERS

kernel_name = 'hybrid_dit_gqa_swa_fnet_peer'


def _rmsnorm(x, w):
    xf = x.astype(jnp.float32)
    y = xf * lax.rsqrt(jnp.mean(xf * xf, axis=-1, keepdims=True) + NORM_EPS)
    return (y * w.astype(jnp.float32)).astype(x.dtype)


def _modulate(x, w, shift, scale):
    return _rmsnorm(x, w) * (1 + scale[:, None, :]) + shift[:, None, :]


def _ada(cond, w, b):
    return jnp.split(jax.nn.silu(cond) @ w + b, N_MOD, axis=-1)


def _axial_rope_tables(row, col, head_dim):
    d_axis = head_dim // 2
    inv_freq = ROPE_THETA ** (-jnp.arange(0, d_axis, 2, dtype=jnp.float32) / d_axis)
    ang = jnp.stack([row.astype(jnp.float32)[:, None] * inv_freq,
                     col.astype(jnp.float32)[:, None] * inv_freq], axis=1)
    return jnp.cos(ang), jnp.sin(ang)


def _apply_axial_rope(x, cos, sin):
    B, S, H, Dh = x.shape
    xs = x.reshape(B, S, H, 2, 2, Dh // 4)
    x1, x2 = xs[..., 0, :], xs[..., 1, :]
    cs = cos[None, :, None].astype(x.dtype)
    sn = sin[None, :, None].astype(x.dtype)
    return jnp.stack([x1 * cs - x2 * sn, x1 * sn + x2 * cs], axis=-2).reshape(B, S, H, Dh)


def _project_qkv(h, w_qkv, n_q, n_kv, head_dim, q_norm, k_norm, need_q=True):
    B, T, _ = h.shape
    qd, kd = n_q * head_dim, n_kv * head_dim
    if need_q:
        qkv = h @ w_qkv
        q = _rmsnorm(qkv[..., :qd].reshape(B, T, n_q, head_dim), q_norm)
        kv = qkv[..., qd:]
    else:
        q = None
        kv = h @ w_qkv[:, qd:]
    k = _rmsnorm(kv[..., :kd].reshape(B, T, n_kv, head_dim), k_norm)
    v = kv[..., kd:].reshape(B, T, n_kv, head_dim)
    return q, k, v


def _attend(q, k, v, bias=None, sink=None):
    s = jnp.einsum('bqhgd,bkhd->bhgqk', q, k).astype(jnp.float32) * (q.shape[-1] ** -0.5)
    if bias is not None:
        s = s + bias
    if sink is None:
        p = jax.nn.softmax(s, axis=-1)
    else:
        sk = jnp.broadcast_to(sink.astype(jnp.float32)[None, :, :, None, None], s.shape[:-1] + (1,))
        p = jax.nn.softmax(jnp.concatenate([s, sk], axis=-1), axis=-1)[..., :-1]
    return jnp.einsum('bhgqk,bkhd->bqhgd', p.astype(v.dtype), v)


def _global_gqa(h_lat, h_ctx, w_qkv, w_o, q_norm, k_norm, cos, sin, with_ctx_out):
    B, S, _ = h_lat.shape
    G = A_Q_HEADS // A_KV_HEADS
    q, k, v = _project_qkv(h_lat, w_qkv, A_Q_HEADS, A_KV_HEADS, A_HEAD_DIM, q_norm, k_norm)
    q = _apply_axial_rope(q, cos, sin)
    k = _apply_axial_rope(k, cos, sin)
    qc, kc, vc = _project_qkv(h_ctx, w_qkv, A_Q_HEADS, A_KV_HEADS, A_HEAD_DIM, q_norm, k_norm, with_ctx_out)
    k_all = jnp.concatenate([kc, k], axis=1)
    v_all = jnp.concatenate([vc, v], axis=1)
    nb = S // Q_BLOCK
    qb = q.reshape(B, nb, Q_BLOCK, A_KV_HEADS, G, A_HEAD_DIM).swapaxes(0, 1)
    o = lax.map(lambda qx: _attend(qx, k_all, v_all), qb)
    y_lat = o.swapaxes(0, 1).reshape(B, S, A_Q_HEADS * A_HEAD_DIM) @ w_o
    y_ctx = None
    if with_ctx_out:
        Tc = kc.shape[1]
        oc = _attend(qc.reshape(B, Tc, A_KV_HEADS, G, A_HEAD_DIM), kc, vc)
        y_ctx = oc.reshape(B, Tc, A_Q_HEADS * A_HEAD_DIM) @ w_o
    return y_lat, y_ctx


def _window_gqa(h_lat, h_ctx, w_qkv, w_o, q_norm, k_norm, sink, cos, sin, with_ctx_out):
    B, S, _ = h_lat.shape
    G = B_Q_HEADS // B_KV_HEADS
    q, k, v = _project_qkv(h_lat, w_qkv, B_Q_HEADS, B_KV_HEADS, B_HEAD_DIM, q_norm, k_norm)
    q = _apply_axial_rope(q, cos, sin)
    k = _apply_axial_rope(k, cos, sin)
    qc, kc, vc = _project_qkv(h_ctx, w_qkv, B_Q_HEADS, B_KV_HEADS, B_HEAD_DIM, q_norm, k_norm, with_ctx_out)
    Tc = kc.shape[1]
    sink = sink.reshape(B_KV_HEADS, G)
    span = Q_BLOCK + 2 * WINDOW
    pad = ((0, 0), (WINDOW, WINDOW), (0, 0), (0, 0))
    kp = jnp.pad(k, pad)
    vp = jnp.pad(v, pad)
    r = jnp.arange(span)
    qi = jnp.arange(Q_BLOCK)
    in_band = jnp.abs(qi[:, None] - r[None, :] + WINDOW) <= WINDOW
    ctx_bias = jnp.zeros((Q_BLOCK, Tc), jnp.float32)
    nb = S // Q_BLOCK
    qb = q.reshape(B, nb, Q_BLOCK, B_KV_HEADS, G, B_HEAD_DIM).swapaxes(0, 1)

    def block(args):
        b, qx = args
        start = b * Q_BLOCK
        kb = lax.dynamic_slice_in_dim(kp, start, span, axis=1)
        vb = lax.dynamic_slice_in_dim(vp, start, span, axis=1)
        key_pos = start - WINDOW + r
        ok = in_band & ((key_pos >= 0) & (key_pos < S))[None, :]
        bias = jnp.concatenate([ctx_bias, jnp.where(ok, 0.0, NEG_INF).astype(jnp.float32)], axis=1)
        return _attend(qx, jnp.concatenate([kc, kb], axis=1), jnp.concatenate([vc, vb], axis=1), bias, sink)

    o = lax.map(block, (jnp.arange(nb), qb))
    y_lat = o.swapaxes(0, 1).reshape(B, S, B_Q_HEADS * B_HEAD_DIM) @ w_o
    y_ctx = None
    if with_ctx_out:
        oc = _attend(qc.reshape(B, Tc, B_KV_HEADS, G, B_HEAD_DIM), kc, vc, None, sink)
        y_ctx = oc.reshape(B, Tc, B_Q_HEADS * B_HEAD_DIM) @ w_o
    return y_lat, y_ctx


def _fourier_mix(h, w_out):
    B, T, D = h.shape
    hg = h.astype(jnp.float32).reshape(B, T, FNET_GROUPS, D // FNET_GROUPS)
    mixed = jnp.fft.fft2(hg, axes=(1, 3), norm='ortho').real
    return mixed.reshape(B, T, D).astype(h.dtype) @ w_out


def _peer(h, w_q, sub_keys, u_tab, v_tab):
    B, T, D = h.shape
    q = (h @ w_q).reshape(B, T, PEER_HEADS, 2, PEER_HALF)
    s = jnp.einsum('bthpk,hpnk->bthpn', q, sub_keys).astype(jnp.float32)
    sv, si = lax.top_k(s, PEER_TOPK)
    cand_s = (sv[..., 0, :, None] + sv[..., 1, None, :]).reshape(B, T, PEER_HEADS, PEER_TOPK * PEER_TOPK)
    cand_e = (si[..., 0, :, None] * PEER_N_KEYS + si[..., 1, None, :]).reshape(B, T, PEER_HEADS, PEER_TOPK * PEER_TOPK)
    top_s, top_pos = lax.top_k(cand_s, PEER_TOPK)
    expert = jnp.take_along_axis(cand_e, top_pos, axis=-1)
    gate = jax.nn.softmax(top_s, axis=-1).astype(h.dtype)
    n_sel = PEER_HEADS * PEER_TOPK
    nblk = (B * T) // PEER_BLOCK
    hb = h.reshape(nblk, PEER_BLOCK, D)
    eb = expert.reshape(nblk, PEER_BLOCK, n_sel)
    gb = gate.reshape(nblk, PEER_BLOCK, n_sel)

    def block(args):
        hx, ex, gx = args
        pre = jnp.einsum('td,tkd->tk', hx, u_tab[ex])
        act = jax.nn.gelu(pre.astype(jnp.float32), approximate=False).astype(hx.dtype)
        return jnp.einsum('tk,tkd->td', gx * act, v_tab[ex])

    return lax.map(block, (hb, eb, gb)).reshape(B, T, D)


def setup_inputs(seed: int = 0) -> dict:
    key = jax.random.key(seed)
    ks = jax.random.split(key, 22)
    D = D_MODEL

    def normal(k, shape, std):
        return jax.random.normal(k, shape, jnp.float32) * std

    a_cols = (A_Q_HEADS + 2 * A_KV_HEADS) * A_HEAD_DIM
    b_cols = (B_Q_HEADS + 2 * B_KV_HEADS) * B_HEAD_DIM
    a_inner = A_Q_HEADS * A_HEAD_DIM
    b_inner = B_Q_HEADS * B_HEAD_DIM
    return {
        'x': normal(ks[0], (BATCH, SEQ, D), 1.0),
        'c': normal(ks[1], (BATCH, D), 1.0),
        'ctx': normal(ks[2], (BATCH, CTX_LEN, D), 1.0),
        'c_ctx': normal(ks[3], (D,), 1.0),
        'ada_w': normal(ks[4], (DEPTH, D, N_MOD * D), 0.5 * D ** -0.5),
        'ada_b': normal(ks[5], (DEPTH, N_MOD * D), 0.01),
        'mix_norm': 1.0 + normal(ks[6], (DEPTH, D), 0.02),
        'ffn_norm': 1.0 + normal(ks[7], (DEPTH, D), 0.02),
        'a_w_qkv': normal(ks[8], (N_A_LAYERS, D, a_cols), D ** -0.5),
        'a_w_o': normal(ks[9], (N_A_LAYERS, a_inner, D), a_inner ** -0.5),
        'a_q_norm': 1.0 + normal(ks[10], (N_A_LAYERS, A_HEAD_DIM), 0.02),
        'a_k_norm': 1.0 + normal(ks[11], (N_A_LAYERS, A_HEAD_DIM), 0.02),
        'b_w_qkv': normal(ks[12], (N_B_LAYERS, D, b_cols), D ** -0.5),
        'b_w_o': normal(ks[13], (N_B_LAYERS, b_inner, D), b_inner ** -0.5),
        'b_q_norm': 1.0 + normal(ks[14], (N_B_LAYERS, B_HEAD_DIM), 0.02),
        'b_k_norm': 1.0 + normal(ks[15], (N_B_LAYERS, B_HEAD_DIM), 0.02),
        'b_sink': normal(ks[16], (N_B_LAYERS, B_Q_HEADS), 0.5),
        'f_w_out': normal(ks[17], (N_C_LAYERS, D, D), D ** -0.5),
        'peer_w_q': normal(ks[18], (DEPTH, D, PEER_HEADS * PEER_QUERY_DIM), D ** -0.5),
        'peer_sub_keys': normal(ks[19], (DEPTH, PEER_HEADS, 2, PEER_N_KEYS, PEER_HALF), PEER_HALF ** -0.5),
        'peer_u': normal(ks[20], (DEPTH, PEER_EXPERTS, D), D ** -0.5),
        'peer_v': normal(ks[21], (DEPTH, PEER_EXPERTS, D), PEER_HEADS ** -0.5),
    }


def reference(x, c, ctx, c_ctx, ada_w, ada_b, mix_norm, ffn_norm, a_w_qkv, a_w_o, a_q_norm, a_k_norm,
              b_w_qkv, b_w_o, b_q_norm, b_k_norm, b_sink, f_w_out, peer_w_q, peer_sub_keys, peer_u, peer_v):
    B, S, D = x.shape
    ROWS = S // GRID_W
    row = jnp.repeat(jnp.arange(ROWS), GRID_W)
    col = jnp.tile(jnp.arange(GRID_W), ROWS)
    cos_a, sin_a = _axial_rope_tables(row, col, A_HEAD_DIM)
    cos_b, sin_b = _axial_rope_tables(row, col, B_HEAD_DIM)
    cx = ctx
    Tc = ctx.shape[1]
    for layer in range(DEPTH):
        last = layer == DEPTH - 1
        kind, j = layer % N_MIXERS, layer // N_MIXERS
        sh1, sc1, g1, sh2, sc2, g2 = _ada(c, ada_w[layer], ada_b[layer])
        csh1, csc1, cg1, csh2, csc2, cg2 = _ada(c_ctx[None, :], ada_w[layer], ada_b[layer])
        hl = _modulate(x, mix_norm[layer], sh1, sc1)
        hc = None if (last and kind == 2) else _modulate(cx, mix_norm[layer], csh1, csc1)
        if kind == 0:
            yl, yc = _global_gqa(hl, hc, a_w_qkv[j], a_w_o[j], a_q_norm[j], a_k_norm[j], cos_a, sin_a, not last)
        elif kind == 1:
            yl, yc = _window_gqa(hl, hc, b_w_qkv[j], b_w_o[j], b_q_norm[j], b_k_norm[j], b_sink[j],
                                 cos_b, sin_b, not last)
        else:
            yl = _fourier_mix(hl, f_w_out[j])
            yc = None if last else _fourier_mix(hc, f_w_out[j])
        x = x + g1[:, None, :] * yl
        hl2 = _modulate(x, ffn_norm[layer], sh2, sc2)
        if last:
            x = x + g2[:, None, :] * _peer(hl2, peer_w_q[layer], peer_sub_keys[layer], peer_u[layer], peer_v[layer])
        else:
            cx = cx + cg1[:, None, :] * yc
            hc2 = _modulate(cx, ffn_norm[layer], csh2, csc2)
            y = _peer(jnp.concatenate([hc2, hl2], axis=1), peer_w_q[layer], peer_sub_keys[layer],
                      peer_u[layer], peer_v[layer])
            cx = cx + cg2[:, None, :] * y[:, :Tc]
            x = x + g2[:, None, :] * y[:, Tc:]
    return x
```

```python
import functools
import math

import numpy as np
import jax
import jax.numpy as jnp
from jax import lax
from jax.experimental import pallas as pl
from jax.experimental.pallas import tpu as pltpu

GRID_W = 64
Q_BLOCK = 128
ROPE_THETA = 10000.0
NORM_EPS = 1e-6
NEG_INF = -1e30
N_MOD = 6
N_MIXERS = 3
A_HEAD_DIM = 128
A_GROUP = 4
B_HEAD_DIM = 64
B_GROUP = 8
WINDOW = 128
FNET_GROUPS = 8
PEER_HEADS = 8
PEER_N_KEYS = 96
PEER_TOPK = 16
PEER_HALF = 128

V7X_LANES = 128
V7X_SUBLANES = 8
V7X_BF16_ROWS = 16
V7X_VMEM_BYTES = 64 * 1024 * 1024
VMEM_LIMIT = V7X_VMEM_BYTES - 8 * 1024 * 1024

MXU_DTYPE = jnp.bfloat16
LOG2E = 1.4426950408889634
RANK_NONE = 99.0


def _params(*sem):
    return pltpu.CompilerParams(dimension_semantics=sem, vmem_limit_bytes=VMEM_LIMIT)


def _pick(n, prefs):
    for p in prefs:
        if n % p == 0:
            return p
    return n


def _ada_kernel(cond_ref, w_ref, b_ref, o_ref):
    w = w_ref[0]
    reps = w.shape[1] // V7X_LANES
    rows = []
    for r in range(2):
        c = cond_ref[r]
        s = c * jax.nn.sigmoid(c)
        rows.append(jnp.sum(w * jnp.tile(s, (1, reps)), axis=0, keepdims=True))
    o_ref[0] = jnp.concatenate(rows, axis=0) + b_ref[0]


def _ada(cond2, ada_w, ada_b):
    depth, d, n = ada_w.shape
    tn = _pick(n, (512, 256, 128))
    cond_rep = jnp.broadcast_to(cond2[:, :, None], (2, d, V7X_LANES))
    return pl.pallas_call(
        _ada_kernel,
        grid=(depth, n // tn),
        in_specs=[pl.BlockSpec((2, d, V7X_LANES), lambda l, j: (0, 0, 0)),
                  pl.BlockSpec((1, d, tn), lambda l, j: (l, 0, j)),
                  pl.BlockSpec((1, 1, tn), lambda l, j: (l, 0, j))],
        out_specs=pl.BlockSpec((1, 2, tn), lambda l, j: (l, 0, j)),
        out_shape=jax.ShapeDtypeStruct((depth, 2, n), jnp.float32),
        compiler_params=_params("parallel", "parallel"),
        name="ada",
    )(cond_rep, ada_w, ada_b.reshape(depth, 1, n))


def _modulate_kernel(x_ref, nw_ref, mod_ref, o_ref, *, transpose):
    x = x_ref[...]
    y = x * lax.rsqrt(jnp.mean(x * x, axis=-1, keepdims=True) + NORM_EPS) * nw_ref[...]
    shift = mod_ref[0, 0:1, :]
    scale = mod_ref[0, 1:2, :]
    h = y * (1.0 + scale) + shift
    if transpose:
        o_ref[...] = h.T.astype(o_ref.dtype)
    else:
        o_ref[...] = h.astype(o_ref.dtype)


def _modulate(x, nw, mod_ctx, mod_lat, tc, *, transpose):
    t, d = x.shape
    tm = _pick(math.gcd(t, tc), (256, 128))
    ctx_tiles = tc // tm
    mods = jnp.stack([mod_ctx, mod_lat])
    out_shape = (d, t) if transpose else (t, d)
    out_spec = (pl.BlockSpec((d, tm), lambda i: (0, i)) if transpose
                else pl.BlockSpec((tm, d), lambda i: (i, 0)))
    return pl.pallas_call(
        functools.partial(_modulate_kernel, transpose=transpose),
        grid=(t // tm,),
        in_specs=[pl.BlockSpec((tm, d), lambda i: (i, 0)),
                  pl.BlockSpec((1, d), lambda i: (0, 0)),
                  pl.BlockSpec((1, 2, d), lambda i: (jnp.where(i < ctx_tiles, 0, 1), 0, 0))],
        out_specs=out_spec,
        out_shape=jax.ShapeDtypeStruct(out_shape, MXU_DTYPE),
        compiler_params=_params("parallel"),
        name="modulate_t" if transpose else "modulate",
    )(x, nw.reshape(1, d), mods)


def _mm_kernel(a_ref, b_ref, o_ref):
    o_ref[...] = jnp.dot(a_ref[...], b_ref[...], preferred_element_type=jnp.float32).astype(o_ref.dtype)


def _mm(a, b, out_dtype, *, tm_prefs=(1280, 1024, 640, 512, 256, 128), tn_prefs=(512, 256, 128), name="mm"):
    m, k = a.shape
    _, n = b.shape
    tm, tn = _pick(m, tm_prefs), _pick(n, tn_prefs)
    return pl.pallas_call(
        _mm_kernel,
        grid=(m // tm, n // tn),
        in_specs=[pl.BlockSpec((tm, k), lambda i, j: (i, 0)),
                  pl.BlockSpec((k, tn), lambda i, j: (0, j))],
        out_specs=pl.BlockSpec((tm, tn), lambda i, j: (i, j)),
        out_shape=jax.ShapeDtypeStruct((m, n), out_dtype),
        compiler_params=_params("parallel", "parallel"),
        name=name,
    )(a, b)


def _mm_res_kernel(a_ref, b_ref, res_ref, g_ref, o_ref, *, tc, transposed):
    acc = jnp.dot(a_ref[...], b_ref[...], preferred_element_type=jnp.float32)
    if transposed:
        acc = acc.T
        row0 = pl.program_id(0) * acc.shape[0]
    else:
        row0 = pl.program_id(0) * acc.shape[0]
    rid = row0 + lax.broadcasted_iota(jnp.int32, acc.shape, 0)
    gate = jnp.where(rid < tc, g_ref[0:1, :], g_ref[1:2, :])
    o_ref[...] = res_ref[...] + gate * acc


def _mm_res(a, b, res, gates, tc, *, name="mm_res"):
    m, k = a.shape
    _, n = b.shape
    tm, tn = _pick(m, (1280, 640, 256, 128)), _pick(n, (512, 256, 128))
    return pl.pallas_call(
        functools.partial(_mm_res_kernel, tc=tc, transposed=False),
        grid=(m // tm, n // tn),
        in_specs=[pl.BlockSpec((tm, k), lambda i, j: (i, 0)),
                  pl.BlockSpec((k, tn), lambda i, j: (0, j)),
                  pl.BlockSpec((tm, tn), lambda i, j: (i, j)),
                  pl.BlockSpec((2, tn), lambda i, j: (0, j))],
        out_specs=pl.BlockSpec((tm, tn), lambda i, j: (i, j)),
        out_shape=jax.ShapeDtypeStruct((m, n), jnp.float32),
        compiler_params=_params("parallel", "parallel"),
        name=name,
    )(a, b, res, gates)


def _mm_res_t(vt, at, res, gates, tc, *, name="mm_res_t"):
    n, k = vt.shape
    _, t = at.shape
    tt, tn = _pick(t, (640, 256, 128)), _pick(n, (512, 256, 128))
    return pl.pallas_call(
        functools.partial(_mm_res_kernel, tc=tc, transposed=True),
        grid=(t // tt, n // tn),
        in_specs=[pl.BlockSpec((tn, k), lambda i, j: (j, 0)),
                  pl.BlockSpec((k, tt), lambda i, j: (0, i)),
                  pl.BlockSpec((tt, tn), lambda i, j: (i, j)),
                  pl.BlockSpec((2, tn), lambda i, j: (0, j))],
        out_specs=pl.BlockSpec((tt, tn), lambda i, j: (i, j)),
        out_shape=jax.ShapeDtypeStruct((t, n), jnp.float32),
        compiler_params=_params("parallel", "parallel"),
        name=name,
    )(vt, at, res, gates)


def _mm_qk_kernel(a_ref, b_ref, nw_ref, cos_ref, sin_ref, o_ref, *, hd):
    acc = jnp.dot(a_ref[...], b_ref[...], preferred_element_type=jnp.float32)
    cos = cos_ref[...]
    sin = sin_ref[...]
    lane = lax.broadcasted_iota(jnp.int32, (acc.shape[0], V7X_LANES), 1)
    for c in range(acc.shape[1] // V7X_LANES):
        blk = acc[:, c * V7X_LANES:(c + 1) * V7X_LANES]
        sq = blk * blk
        if hd == V7X_LANES:
            ms = jnp.sum(sq, axis=-1, keepdims=True) * (1.0 / hd)
        else:
            lo = jnp.sum(jnp.where(lane < hd, sq, 0.0), axis=-1, keepdims=True)
            hi = jnp.sum(jnp.where(lane < hd, 0.0, sq), axis=-1, keepdims=True)
            ms = jnp.where(lane < hd, lo, hi) * (1.0 / hd)
        y = blk * lax.rsqrt(ms + NORM_EPS) * nw_ref[:, c * V7X_LANES:(c + 1) * V7X_LANES]
        if hd == V7X_LANES:
            partner = pltpu.roll(y, hd // 2, 1)
        else:
            partner = jnp.where(lane % hd < hd // 2,
                                pltpu.roll(y, V7X_LANES - hd // 2, 1), pltpu.roll(y, hd // 2, 1))
        o_ref[:, c * V7X_LANES:(c + 1) * V7X_LANES] = (y * cos + partner * sin).astype(o_ref.dtype)


def _mm_qk(a, b, nw, cos, sin, hd, *, name):
    m, k = a.shape
    _, n = b.shape
    tm, tn = _pick(m, (1280, 640, 256, 128)), _pick(n, (512, 256, 128))
    return pl.pallas_call(
        functools.partial(_mm_qk_kernel, hd=hd),
        grid=(m // tm, n // tn),
        in_specs=[pl.BlockSpec((tm, k), lambda i, j: (i, 0)),
                  pl.BlockSpec((k, tn), lambda i, j: (0, j)),
                  pl.BlockSpec((1, tn), lambda i, j: (0, j)),
                  pl.BlockSpec((tm, V7X_LANES), lambda i, j: (i, 0)),
                  pl.BlockSpec((tm, V7X_LANES), lambda i, j: (i, 0))],
        out_specs=pl.BlockSpec((tm, tn), lambda i, j: (i, j)),
        out_shape=jax.ShapeDtypeStruct((m, n), MXU_DTYPE),
        compiler_params=_params("parallel", "parallel"),
        name=name,
    )(a, b, nw, cos, sin)


def _flash_kernel(q_ref, k_ref, v_ref, m0_ref, o_ref, m_ref, acc_ref, *, hd, init_l):
    nchunk = k_ref.shape[1]
    q = q_ref[0]
    m_ref[...] = m0_ref[0]
    row = lax.broadcasted_iota(jnp.int32, acc_ref.shape, 0)
    acc_ref[...] = jnp.where(row == hd, init_l, 0.0).astype(jnp.float32)

    def chunk(c, carry):
        s = jnp.dot(k_ref[0, c], q, preferred_element_type=jnp.float32)
        m_old = m_ref[...]
        m_new = jnp.maximum(m_old, jnp.max(s, axis=0, keepdims=True))
        alpha = jnp.exp2(m_old - m_new)
        p = jnp.exp2(s - m_new).astype(MXU_DTYPE)
        acc_ref[...] = acc_ref[...] * alpha + jnp.dot(v_ref[0, c], p, preferred_element_type=jnp.float32)
        m_ref[...] = m_new
        return carry

    lax.fori_loop(0, nchunk, chunk, 0)
    acc = acc_ref[...]
    o_ref[0] = (acc[:hd] / acc[hd:hd + 1]).astype(o_ref.dtype)


def _flash(qt, kc, vc, m0, hd, m_lanes, init_l, *, name):
    kvh, _, nq = qt.shape
    _, nchunk, tk, _ = kc.shape
    hv = vc.shape[2]
    return pl.pallas_call(
        functools.partial(_flash_kernel, hd=hd, init_l=init_l),
        grid=(kvh, nq // m_lanes),
        in_specs=[pl.BlockSpec((1, hd, m_lanes), lambda h, i: (h, 0, i)),
                  pl.BlockSpec((1, nchunk, tk, hd), lambda h, i: (h, 0, 0, 0)),
                  pl.BlockSpec((1, nchunk, hv, tk), lambda h, i: (h, 0, 0, 0)),
                  pl.BlockSpec((1, 1, m_lanes), lambda h, i: (h, 0, 0))],
        out_specs=pl.BlockSpec((1, hd, m_lanes), lambda h, i: (h, 0, i)),
        out_shape=jax.ShapeDtypeStruct((kvh, hd, nq), MXU_DTYPE),
        scratch_shapes=[pltpu.VMEM((1, m_lanes), jnp.float32),
                        pltpu.VMEM((hv, m_lanes), jnp.float32)],
        compiler_params=_params("parallel", "arbitrary"),
        name=name,
    )(qt, kc, vc, m0)


def _window_kernel(q_ref, kc_ref, vc_ref, kp_ref, kb_ref, kn_ref, vp_ref, vb_ref, vn_ref, sink_ref, o_ref,
                   *, hd, tc, seq):
    q = q_ref[0]
    kcat = jnp.concatenate([kc_ref[0], kp_ref[0], kb_ref[0], kn_ref[0]], axis=0)
    vcat = jnp.concatenate([vc_ref[0], vp_ref[0], vb_ref[0], vn_ref[0]], axis=1)
    s = jnp.dot(kcat, q, preferred_element_type=jnp.float32)
    start = pl.program_id(1) * Q_BLOCK
    r = lax.broadcasted_iota(jnp.int32, s.shape, 0)
    lane = lax.broadcasted_iota(jnp.int32, s.shape, 1)
    key_pos = start - WINDOW + (r - tc)
    q_pos = start + lane % Q_BLOCK
    in_band = (jnp.abs(q_pos - key_pos) <= WINDOW) & (key_pos >= 0) & (key_pos < seq)
    s = jnp.where((r < tc) | in_band, s, NEG_INF)
    sink = sink_ref[0]
    m = jnp.maximum(sink, jnp.max(s, axis=0, keepdims=True))
    p = jnp.exp2(s - m).astype(MXU_DTYPE)
    acc = jnp.dot(vcat, p, preferred_element_type=jnp.float32)
    l = acc[hd:hd + 1] + jnp.exp2(sink - m)
    o_ref[0] = (acc[:hd] / l).astype(o_ref.dtype)


def _window(qt, kc, vc, kl, vl, sink_m, hd, m_lanes, tc, seq):
    kvh, _, nq = qt.shape
    hv = vc.shape[1]
    nb = seq // Q_BLOCK
    kspec = lambda f: pl.BlockSpec((1, Q_BLOCK, hd), lambda h, b: (h, f(b), 0))
    vspec = lambda f: pl.BlockSpec((1, hv, Q_BLOCK), lambda h, b: (h, 0, f(b)))
    prev = lambda b: jnp.maximum(b - 1, 0)
    nxt = lambda b: jnp.minimum(b + 1, nb - 1)
    cur = lambda b: b
    return pl.pallas_call(
        functools.partial(_window_kernel, hd=hd, tc=tc, seq=seq),
        grid=(kvh, nb),
        in_specs=[pl.BlockSpec((1, hd, m_lanes), lambda h, b: (h, 0, b)),
                  pl.BlockSpec((1, tc, hd), lambda h, b: (h, 0, 0)),
                  pl.BlockSpec((1, hv, tc), lambda h, b: (h, 0, 0)),
                  kspec(prev), kspec(cur), kspec(nxt), vspec(prev), vspec(cur), vspec(nxt),
                  pl.BlockSpec((1, 1, m_lanes), lambda h, b: (h, 0, 0))],
        out_specs=pl.BlockSpec((1, hd, m_lanes), lambda h, b: (h, 0, b)),
        out_shape=jax.ShapeDtypeStruct((kvh, hd, nq), MXU_DTYPE),
        compiler_params=_params("parallel", "parallel"),
        name="window_attn",
    )(qt, kc, vc, kl, kl, kl, vl, vl, vl, sink_m)


def _rope_tables(seq, tc, hd):
    d_axis = hd // 2
    inv_freq = ROPE_THETA ** (-jnp.arange(0, d_axis, 2, dtype=jnp.float32) / d_axis)
    t = jnp.arange(seq)
    row = (t // GRID_W).astype(jnp.float32)[:, None] * inv_freq
    col = (t % GRID_W).astype(jnp.float32)[:, None] * inv_freq
    ang = jnp.concatenate([row, col], axis=1)
    cos = jnp.concatenate([jnp.cos(ang), jnp.cos(ang)], axis=1)
    sin = jnp.concatenate([-jnp.sin(ang), jnp.sin(ang)], axis=1)
    reps = V7X_LANES // hd
    cos = jnp.tile(cos, (1, reps))
    sin = jnp.tile(sin, (1, reps))
    cos = jnp.concatenate([jnp.ones((tc, V7X_LANES), jnp.float32), cos], axis=0)
    sin = jnp.concatenate([jnp.zeros((tc, V7X_LANES), jnp.float32), sin], axis=0)
    return cos, sin


def _head_perm(hd):
    f = hd // 4
    idx = np.arange(hd).reshape(2, 2, f)
    return idx.transpose(1, 0, 2).reshape(hd)


def _to_qt(q, kvh, g, hd, tq):
    n = q.shape[0]
    return q.reshape(n // tq, tq, kvh, g, hd).transpose(2, 4, 0, 3, 1).reshape(kvh, hd, n * g)


def _from_ot(ot, kvh, g, hd, tq):
    n = ot.shape[2] // g
    return ot.reshape(kvh, hd, n // tq, g, tq).transpose(2, 4, 0, 3, 1).reshape(n, kvh * g * hd)


def _v_aug(v, kvh, hd):
    n = v.shape[0]
    vt = v.reshape(n, kvh, hd).transpose(1, 2, 0)
    ones = jnp.ones((kvh, 1, n), v.dtype)
    zeros = jnp.zeros((kvh, V7X_BF16_ROWS - 1, n), v.dtype)
    return jnp.concatenate([vt, ones, zeros], axis=1)


def _chunk_keys(k, vt, tk):
    kvh, n, hd = k.shape
    kc = k.reshape(kvh, n // tk, tk, hd)
    vc = vt.reshape(kvh, vt.shape[1], n // tk, tk).transpose(0, 2, 1, 3)
    return kc, vc


def _attention_layer(h, w_qkv, w_o, q_norm, k_norm, sink, hd, g, tc, seq, windowed, with_ctx_out):
    t, d = h.shape
    nheads = d // hd
    kvh = nheads // g
    qd, kd = nheads * hd, kvh * hd
    perm = _head_perm(hd)
    wq = w_qkv[:, :qd].reshape(d, nheads, hd)[:, :, perm].reshape(d, qd).astype(MXU_DTYPE)
    wk = w_qkv[:, qd:qd + kd].reshape(d, kvh, hd)[:, :, perm].reshape(d, kd).astype(MXU_DTYPE)
    wv = w_qkv[:, qd + kd:].astype(MXU_DTYPE)
    qscale = (hd ** -0.5) * LOG2E
    nwq = jnp.tile(q_norm[perm] * qscale, nheads).reshape(1, qd)
    nwk = jnp.tile(k_norm[perm], kvh).reshape(1, kd)
    cos, sin = _rope_tables(seq, tc, hd)
    q = _mm_qk(h, wq, nwq, cos, sin, hd, name="proj_q")
    k = _mm_qk(h, wk, nwk, cos, sin, hd, name="proj_k")
    v = _mm(h, wv, MXU_DTYPE, name="proj_v")

    m_lanes = 1024
    tq = m_lanes // g
    kk = k.reshape(t, kvh, hd).transpose(1, 0, 2)
    vt = _v_aug(v, kvh, hd)
    qt_lat = _to_qt(q[tc:], kvh, g, hd, tq)
    if sink is None:
        m_init = jnp.full((kvh, 1, m_lanes), NEG_INF, jnp.float32)
        init_l = 0.0
    else:
        m_init = jnp.repeat(sink.reshape(kvh, g).astype(jnp.float32) * LOG2E, tq, axis=1).reshape(kvh, 1, m_lanes)
        init_l = 1.0
    if windowed:
        o_lat = _window(qt_lat, kk[:, :tc], vt[:, :, :tc], kk[:, tc:], vt[:, :, tc:], m_init, hd, m_lanes, tc, seq)
    else:
        tk = _pick(t, (640, 512, 256, 128))
        kc, vc = _chunk_keys(kk, vt, tk)
        o_lat = _flash(qt_lat, kc, vc, m_init, hd, m_lanes, init_l, name="global_attn")
    o_lat = _from_ot(o_lat, kvh, g, hd, tq)
    if with_ctx_out:
        kc, vc = _chunk_keys(kk[:, :tc], vt[:, :, :tc], tc)
        o_ctx = _flash(_to_qt(q[:tc], kvh, g, hd, tq), kc, vc, m_init, hd, m_lanes, init_l, name="ctx_attn")
        o_ctx = _from_ot(o_ctx, kvh, g, hd, tq)
    else:
        o_ctx = jnp.zeros((tc, qd), MXU_DTYPE)
    return jnp.concatenate([o_ctx, o_lat], axis=0), w_o.astype(MXU_DTYPE)


def _dft_parts(n, scale):
    k = np.arange(n)
    ang = 2.0 * np.pi * ((k[:, None] * k[None, :]) % n) / n
    return np.cos(ang) * scale, -np.sin(ang) * scale


def _chan_dft_kernel(x_ref, c_ref, s_ref, o_ref):
    x = x_ref[...]
    o_ref[0] = jnp.dot(x, c_ref[...], preferred_element_type=jnp.float32).astype(o_ref.dtype)
    o_ref[1] = jnp.dot(x, s_ref[...], preferred_element_type=jnp.float32).astype(o_ref.dtype)


def _chan_dft(h, gc):
    t, d = h.shape
    re, im = _dft_parts(gc, gc ** -0.5)
    tm = _pick(t, (1280, 640, 256, 128))
    return pl.pallas_call(
        _chan_dft_kernel,
        grid=(t // tm, d // gc),
        in_specs=[pl.BlockSpec((tm, gc), lambda i, j: (i, j)),
                  pl.BlockSpec((gc, gc), lambda i, j: (0, 0)),
                  pl.BlockSpec((gc, gc), lambda i, j: (0, 0))],
        out_specs=pl.BlockSpec((2, tm, gc), lambda i, j: (0, i, j)),
        out_shape=jax.ShapeDtypeStruct((2, t, d), MXU_DTYPE),
        compiler_params=_params("parallel", "parallel"),
        name="chan_dft",
    )(h, jnp.asarray(re, MXU_DTYPE), jnp.asarray(im, MXU_DTYPE))


def _pos_stage1_kernel(m_ref, y_ref, tr_ref, ti_ref, o_ref, *, n1):
    y = jnp.concatenate([y_ref[0], y_ref[1]], axis=0)
    a = jnp.dot(m_ref[...], y, preferred_element_type=jnp.float32)
    ar, ai = a[:n1], a[n1:]
    tr, ti = tr_ref[0][:, 0:1], ti_ref[0][:, 0:1]
    o_ref[0] = (ar * tr - ai * ti).astype(o_ref.dtype)
    o_ref[1] = (ar * ti + ai * tr).astype(o_ref.dtype)


def _pos_stage2_kernel(m_ref, b_ref, o_ref):
    b = jnp.concatenate([b_ref[0, 0], b_ref[1, 0]], axis=0)
    o_ref[0] = jnp.dot(m_ref[...], b, preferred_element_type=jnp.float32).astype(o_ref.dtype)


def _pos_dft_two_stage(y, n1, n2):
    _, s, d = y.shape
    sc = float(s) ** -0.25
    r1, i1 = _dft_parts(n1, sc)
    m1 = np.block([[r1, -i1], [i1, r1]])
    f1 = np.arange(n1)[None, :]
    t2 = np.arange(n2)[:, None]
    ang = 2.0 * np.pi * (t2 * f1) / s
    tw_r = np.broadcast_to(np.cos(ang)[:, :, None], (n2, n1, V7X_LANES))
    tw_i = np.broadcast_to(-np.sin(ang)[:, :, None], (n2, n1, V7X_LANES))
    y4 = y.reshape(2, n1, n2 * d)
    b = pl.pallas_call(
        functools.partial(_pos_stage1_kernel, n1=n1),
        grid=(n2,),
        in_specs=[pl.BlockSpec((2 * n1, 2 * n1), lambda j: (0, 0)),
                  pl.BlockSpec((2, n1, d), lambda j: (0, 0, j)),
                  pl.BlockSpec((1, n1, V7X_LANES), lambda j: (j, 0, 0)),
                  pl.BlockSpec((1, n1, V7X_LANES), lambda j: (j, 0, 0))],
        out_specs=pl.BlockSpec((2, n1, d), lambda j: (0, 0, j)),
        out_shape=jax.ShapeDtypeStruct((2, n1, n2 * d), MXU_DTYPE),
        compiler_params=_params("parallel"),
        name="pos_dft_stage1",
    )(jnp.asarray(m1, MXU_DTYPE), y4, jnp.asarray(tw_r, jnp.float32), jnp.asarray(tw_i, jnp.float32))
    b4 = b.reshape(2, n1, n2, d)
    r2, i2 = _dft_parts(n2, sc)
    m2 = np.concatenate([r2, -i2], axis=1)
    tcn = _pick(d, (2048, 1024, 512, 256, 128))
    out = pl.pallas_call(
        _pos_stage2_kernel,
        grid=(n1, d // tcn),
        in_specs=[pl.BlockSpec((n2, 2 * n2), lambda f, j: (0, 0)),
                  pl.BlockSpec((2, 1, n2, tcn), lambda f, j: (0, f, 0, j))],
        out_specs=pl.BlockSpec((1, n2, tcn), lambda f, j: (f, 0, j)),
        out_shape=jax.ShapeDtypeStruct((n1, n2, d), MXU_DTYPE),
        compiler_params=_params("parallel", "parallel"),
        name="pos_dft_stage2",
    )(jnp.asarray(m2, MXU_DTYPE), b4)
    return out.transpose(1, 0, 2).reshape(s, d)


def _fourier_mix(h, tc, seq):
    t, d = h.shape
    y = _chan_dft(h, d // FNET_GROUPS)
    rc, ic = _dft_parts(tc, tc ** -0.5)
    mc = np.concatenate([rc, -ic], axis=1)
    yc = jnp.concatenate([y[0, :tc], y[1, :tc]], axis=0)
    mixed_c = _mm(jnp.asarray(mc, MXU_DTYPE), yc, MXU_DTYPE, name="ctx_pos_dft")
    n1 = 1 << (int(math.log2(seq)) // 2)
    n2 = seq // n1
    mixed_l = _pos_dft_two_stage(y[:, tc:], n1, n2)
    return jnp.concatenate([mixed_c, mixed_l], axis=0)


def _top16(s):
    n = s.shape[0]
    rows = lax.broadcasted_iota(jnp.int32, s.shape, 0)
    row16 = lax.broadcasted_iota(jnp.int32, (PEER_TOPK, s.shape[1]), 0)
    rank = jnp.full(s.shape, RANK_NONE, jnp.float32)
    sv = jnp.zeros((PEER_TOPK, s.shape[1]), jnp.float32)
    v = s
    for k in range(PEER_TOPK):
        m = jnp.max(v, axis=0, keepdims=True)
        idx = jnp.min(jnp.where(v == m, rows, n), axis=0, keepdims=True)
        hit = rows == idx
        rank = jnp.where(hit, float(k), rank)
        v = jnp.where(hit, -jnp.inf, v)
        sv = jnp.where(row16 == k, m, sv)
    return rank, sv


_CAND_GROUPS = ((0, 0, 8), (0, 8, 8), (1, 0, 8), (2, 0, 5), (3, 0, 4), (4, 0, 3), (5, 0, 2), (6, 0, 2), (7, 0, 2))


def _select_pairs(sv0, sv1):
    lanes = sv0.shape[1]
    sub = lax.broadcasted_iota(jnp.int32, (V7X_SUBLANES, lanes), 0)
    cands, poss = [], []
    for r0, r1_0, cnt in _CAND_GROUPS:
        c = sv0[r0:r0 + 1] + sv1[r1_0:r1_0 + V7X_SUBLANES]
        cands.append(jnp.where(sub < cnt, c, -jnp.inf))
        poss.append(r0 * PEER_TOPK + r1_0 + sub)
    cands.append(sv0[V7X_SUBLANES:] + sv1[0:1])
    poss.append((V7X_SUBLANES + sub) * PEER_TOPK)
    cand = jnp.concatenate(cands, axis=0)
    pos = jnp.concatenate(poss, axis=0)
    cmax = sv0[0:1] + sv1[0:1]
    big = PEER_TOPK * PEER_TOPK
    c = cand
    sel = jnp.zeros(cand.shape, jnp.float32)
    for _ in range(PEER_TOPK):
        m = jnp.max(c, axis=0, keepdims=True)
        pidx = jnp.min(jnp.where(c == m, pos, big), axis=0, keepdims=True)
        hit = pos == pidx
        sel = jnp.where(hit, 1.0, sel)
        c = jnp.where(hit, -jnp.inf, c)
    z = jnp.sum(jnp.where(sel > 0.0, jnp.exp(cand - cmax), 0.0), axis=0, keepdims=True)
    g = V7X_SUBLANES
    rows = [jnp.sum(sel[0:2 * g], axis=0, keepdims=True)]
    for i in range(2, len(_CAND_GROUPS)):
        rows.append(jnp.sum(sel[i * g:(i + 1) * g], axis=0, keepdims=True))
    rows.append(sel[len(_CAND_GROUPS) * g:])
    return jnp.concatenate(rows, axis=0), z


def _peer_select_kernel(q_ref, keys_ref, r1_ref, b_ref, cnt_ref, a_ref):
    def head(hh, carry):
        base = pl.multiple_of(hh * 2 * PEER_HALF, 2 * PEER_HALF)
        q0 = q_ref[pl.ds(base, PEER_HALF), :]
        q1 = q_ref[pl.ds(base + PEER_HALF, PEER_HALF), :]
        s0 = jnp.dot(keys_ref[hh, 0], q0, preferred_element_type=jnp.float32, precision=lax.Precision.HIGHEST)
        s1 = jnp.dot(keys_ref[hh, 1], q1, preferred_element_type=jnp.float32, precision=lax.Precision.HIGHEST)
        rank0, sv0 = _top16(s0)
        rank1, sv1 = _top16(s1)
        cnt16, z = _select_pairs(sv0, sv1)
        cnt = jnp.zeros(s0.shape, jnp.float32)
        for r in range(PEER_TOPK):
            cnt = jnp.where(rank0 == float(r), cnt16[r:r + 1], cnt)
        r1_ref[hh] = rank1
        b_ref[hh] = jnp.exp(s1 - sv1[0:1])
        cnt_ref[hh] = cnt
        a_ref[hh] = jnp.exp(s0 - sv0[0:1]) / z
        return carry

    lax.fori_loop(0, PEER_HEADS, head, 0)


def _peer_select(qt, sub_keys):
    _, t = qt.shape
    tl = V7X_LANES
    shp = jax.ShapeDtypeStruct((PEER_HEADS, PEER_N_KEYS, t), jnp.float32)
    ospec = pl.BlockSpec((PEER_HEADS, PEER_N_KEYS, tl), lambda i: (0, 0, i))
    return pl.pallas_call(
        _peer_select_kernel,
        grid=(t // tl,),
        in_specs=[pl.BlockSpec((qt.shape[0], tl), lambda i: (0, i)),
                  pl.BlockSpec(sub_keys.shape, lambda i: (0, 0, 0, 0))],
        out_specs=[ospec] * 4,
        out_shape=[shp] * 4,
        compiler_params=_params("parallel"),
        name="peer_select",
    )(qt, sub_keys)


def _gelu_exact(x):
    return 0.5 * x * (1.0 + lax.erf(x * (2.0 ** -0.5)))


def _peer_act_kernel(u_ref, h_ref, r1_ref, b_ref, cnt_ref, a_ref, o_ref, *, iblocks):
    pre = jnp.dot(u_ref[...], h_ref[...], preferred_element_type=jnp.float32)
    nk = PEER_N_KEYS
    for ii in range(iblocks):
        w = jnp.zeros((nk, pre.shape[1]), jnp.float32)
        for hh in range(PEER_HEADS):
            sel = r1_ref[hh] < cnt_ref[hh, ii:ii + 1, :]
            w = w + jnp.where(sel, b_ref[hh], 0.0) * a_ref[hh, ii:ii + 1, :]
        o_ref[ii * nk:(ii + 1) * nk, :] = (_gelu_exact(pre[ii * nk:(ii + 1) * nk]) * w).astype(o_ref.dtype)


def _peer_act(u, ht, r1, b, cnt, a):
    e, d = u.shape
    _, t = ht.shape
    iblocks = V7X_SUBLANES
    te = iblocks * PEER_N_KEYS
    tt = _pick(t, (640, 256, 128))
    full = pl.BlockSpec((PEER_HEADS, PEER_N_KEYS, tt), lambda i, j: (0, 0, i))
    part = pl.BlockSpec((PEER_HEADS, iblocks, tt), lambda i, j: (0, j, i))
    return pl.pallas_call(
        functools.partial(_peer_act_kernel, iblocks=iblocks),
        grid=(t // tt, e // te),
        in_specs=[pl.BlockSpec((te, d), lambda i, j: (j, 0)),
                  pl.BlockSpec((d, tt), lambda i, j: (0, i)),
                  full, full, part, part],
        out_specs=pl.BlockSpec((te, tt), lambda i, j: (j, i)),
        out_shape=jax.ShapeDtypeStruct((e, t), MXU_DTYPE),
        compiler_params=_params("parallel", "parallel"),
        name="peer_act",
    )(u, ht, r1, b, cnt, a)


def _peer(x, nw, mod_ctx, mod_lat, gates, tc, w_q, sub_keys, u_tab, v_tab):
    ht = _modulate(x, nw, mod_ctx, mod_lat, tc, transpose=True)
    qt = _mm(w_q.T.astype(MXU_DTYPE), ht, jnp.float32, tm_prefs=(512, 256, 128),
             tn_prefs=(640, 256, 128), name="peer_query")
    r1, b, cnt, a = _peer_select(qt, sub_keys)
    act = _peer_act(u_tab.astype(MXU_DTYPE), ht, r1, b, cnt, a)
    return _mm_res_t(v_tab.T.astype(MXU_DTYPE), act, x, gates, tc, name="peer_out")


def kernel(x, c, ctx, c_ctx, ada_w, ada_b, mix_norm, ffn_norm, a_w_qkv, a_w_o, a_q_norm, a_k_norm,
           b_w_qkv, b_w_o, b_q_norm, b_k_norm, b_sink, f_w_out, peer_w_q, peer_sub_keys, peer_u, peer_v):
    batch, seq, d = x.shape
    assert batch == 1 and ctx.shape[0] == 1
    tc = ctx.shape[1]
    depth = ada_w.shape[0]
    xs = jnp.concatenate([ctx[0], x[0]], axis=0)
    mods = _ada(jnp.stack([c[0], c_ctx]), ada_w, ada_b)
    mods = mods.reshape(depth, 2, N_MOD, d)
    for layer in range(depth):
        last = layer == depth - 1
        kind, j = layer % N_MIXERS, layer // N_MIXERS
        lat, cx = mods[layer, 0], mods[layer, 1]
        h = _modulate(xs, mix_norm[layer], cx[0:2], lat[0:2], tc, transpose=False)
        if kind == 0:
            y, w_out = _attention_layer(h, a_w_qkv[j], a_w_o[j], a_q_norm[j], a_k_norm[j], None,
                                        A_HEAD_DIM, A_GROUP, tc, seq, False, not last)
        elif kind == 1:
            y, w_out = _attention_layer(h, b_w_qkv[j], b_w_o[j], b_q_norm[j], b_k_norm[j], b_sink[j],
                                        B_HEAD_DIM, B_GROUP, tc, seq, True, not last)
        else:
            y, w_out = _fourier_mix(h, tc, seq), f_w_out[j].astype(MXU_DTYPE)
        xs = _mm_res(y, w_out, xs, jnp.stack([cx[2], lat[2]]), tc, name="mixer_out")
        xs = _peer(xs, ffn_norm[layer], cx[3:5], lat[3:5], jnp.stack([cx[5], lat[5]]), tc,
                   peer_w_q[layer], peer_sub_keys[layer], peer_u[layer], peer_v[layer])
    return xs[tc:][None]
```

```python
import functools
import math

import numpy as np
import jax
import jax.numpy as jnp
from jax import lax
from jax.experimental import pallas as pl
from jax.experimental.pallas import tpu as pltpu

GRID_W = 64
Q_BLOCK = 128
ROPE_THETA = 10000.0
NORM_EPS = 1e-6
NEG_INF = -1e30
N_MOD = 6
N_MIXERS = 3
A_HEAD_DIM = 128
A_GROUP = 4
B_HEAD_DIM = 64
B_GROUP = 8
WINDOW = 128
FNET_GROUPS = 8
PEER_HEADS = 8
PEER_N_KEYS = 96
PEER_TOPK = 16
PEER_HALF = 128

V7X_LANES = 128
V7X_SUBLANES = 8
V7X_BF16_ROWS = 16
MXU_COLS = 256
V7X_VMEM_BYTES = 64 * 1024 * 1024
VMEM_LIMIT = V7X_VMEM_BYTES - 8 * 1024 * 1024

MXU_DTYPE = jnp.bfloat16
LOG2E = 1.4426950408889634
RANK_NONE = 99.0
ROW_TILES = (1536, 768, 512, 256, 128)
TOKEN_TILES = (768, 512, 256)
EXP2_HEADROOM = 64.0


def _params(*sem):
    return pltpu.CompilerParams(dimension_semantics=sem, vmem_limit_bytes=VMEM_LIMIT)


def _pick(n, prefs):
    for p in prefs:
        if n % p == 0:
            return p
    return n


def _ada_kernel(cond_ref, w_ref, b_ref, o_ref):
    w = w_ref[0]
    reps = w.shape[1] // V7X_LANES
    rows = []
    for r in range(2):
        c = cond_ref[r]
        s = c * jax.nn.sigmoid(c)
        rows.append(jnp.sum(w * jnp.tile(s, (1, reps)), axis=0, keepdims=True))
    o_ref[0] = jnp.concatenate(rows, axis=0) + b_ref[0]


def _ada(cond2, ada_w, ada_b):
    depth, d, n = ada_w.shape
    tn = _pick(n, (512, 256, 128))
    cond_rep = jnp.broadcast_to(cond2[:, :, None], (2, d, V7X_LANES))
    return pl.pallas_call(
        _ada_kernel,
        grid=(depth, n // tn),
        in_specs=[pl.BlockSpec((2, d, V7X_LANES), lambda l, j: (0, 0, 0)),
                  pl.BlockSpec((1, d, tn), lambda l, j: (l, 0, j)),
                  pl.BlockSpec((1, 1, tn), lambda l, j: (l, 0, j))],
        out_specs=pl.BlockSpec((1, 2, tn), lambda l, j: (l, 0, j)),
        out_shape=jax.ShapeDtypeStruct((depth, 2, n), jnp.float32),
        compiler_params=_params("parallel", "parallel"),
        name="ada",
    )(cond_rep, ada_w, ada_b.reshape(depth, 1, n))


def _modulate_kernel(x_ref, nw_ref, mod_ref, o_ref, *, transpose):
    x = x_ref[...]
    y = x * lax.rsqrt(jnp.mean(x * x, axis=-1, keepdims=True) + NORM_EPS) * nw_ref[...]
    shift = mod_ref[0, 0:1, :]
    scale = mod_ref[0, 1:2, :]
    h = y * (1.0 + scale) + shift
    if transpose:
        o_ref[...] = h.T.astype(o_ref.dtype)
    else:
        o_ref[...] = h.astype(o_ref.dtype)


def _modulate(x, nw, mod_ctx, mod_lat, tc, *, transpose):
    t, d = x.shape
    tm = _pick(math.gcd(t, tc), (256, 128))
    ctx_tiles = tc // tm
    mods = jnp.stack([mod_ctx, mod_lat])
    out_shape = (d, t) if transpose else (t, d)
    out_spec = (pl.BlockSpec((d, tm), lambda i: (0, i)) if transpose
                else pl.BlockSpec((tm, d), lambda i: (i, 0)))
    return pl.pallas_call(
        functools.partial(_modulate_kernel, transpose=transpose),
        grid=(t // tm,),
        in_specs=[pl.BlockSpec((tm, d), lambda i: (i, 0)),
                  pl.BlockSpec((1, d), lambda i: (0, 0)),
                  pl.BlockSpec((1, 2, d), lambda i: (jnp.where(i < ctx_tiles, 0, 1), 0, 0))],
        out_specs=out_spec,
        out_shape=jax.ShapeDtypeStruct(out_shape, MXU_DTYPE),
        compiler_params=_params("parallel"),
        name="modulate_t" if transpose else "modulate",
    )(x, nw.reshape(1, d), mods)


def _mm_kernel(a_ref, b_ref, o_ref):
    o_ref[...] = jnp.dot(a_ref[...], b_ref[...], preferred_element_type=jnp.float32).astype(o_ref.dtype)


def _mm(a, b, out_dtype, *, tm_prefs=ROW_TILES, tn_prefs=(512, 256, 128), name="mm"):
    m, k = a.shape
    _, n = b.shape
    tm, tn = _pick(m, tm_prefs), _pick(n, tn_prefs)
    return pl.pallas_call(
        _mm_kernel,
        grid=(m // tm, n // tn),
        in_specs=[pl.BlockSpec((tm, k), lambda i, j: (i, 0)),
                  pl.BlockSpec((k, tn), lambda i, j: (0, j))],
        out_specs=pl.BlockSpec((tm, tn), lambda i, j: (i, j)),
        out_shape=jax.ShapeDtypeStruct((m, n), out_dtype),
        compiler_params=_params("parallel", "parallel"),
        name=name,
    )(a, b)


def _mm_res_kernel(a_ref, b_ref, res_ref, g_ref, o_ref, *, tc, transposed):
    acc = jnp.dot(a_ref[...], b_ref[...], preferred_element_type=jnp.float32)
    if transposed:
        acc = acc.T
        row0 = pl.program_id(0) * acc.shape[0]
    else:
        row0 = pl.program_id(0) * acc.shape[0]
    rid = row0 + lax.broadcasted_iota(jnp.int32, acc.shape, 0)
    gate = jnp.where(rid < tc, g_ref[0:1, :], g_ref[1:2, :])
    o_ref[...] = res_ref[...] + gate * acc


def _mm_res(a, b, res, gates, tc, *, name="mm_res"):
    m, k = a.shape
    _, n = b.shape
    tm, tn = _pick(m, ROW_TILES), _pick(n, (512, 256, 128))
    return pl.pallas_call(
        functools.partial(_mm_res_kernel, tc=tc, transposed=False),
        grid=(m // tm, n // tn),
        in_specs=[pl.BlockSpec((tm, k), lambda i, j: (i, 0)),
                  pl.BlockSpec((k, tn), lambda i, j: (0, j)),
                  pl.BlockSpec((tm, tn), lambda i, j: (i, j)),
                  pl.BlockSpec((2, tn), lambda i, j: (0, j))],
        out_specs=pl.BlockSpec((tm, tn), lambda i, j: (i, j)),
        out_shape=jax.ShapeDtypeStruct((m, n), jnp.float32),
        compiler_params=_params("parallel", "parallel"),
        name=name,
    )(a, b, res, gates)


def _mm_res_t(vt, at, res, gates, tc, *, name="mm_res_t"):
    n, k = vt.shape
    _, t = at.shape
    tt, tn = _pick(t, TOKEN_TILES), _pick(n, (256, 128))
    return pl.pallas_call(
        functools.partial(_mm_res_kernel, tc=tc, transposed=True),
        grid=(t // tt, n // tn),
        in_specs=[pl.BlockSpec((tn, k), lambda i, j: (j, 0)),
                  pl.BlockSpec((k, tt), lambda i, j: (0, i)),
                  pl.BlockSpec((tt, tn), lambda i, j: (i, j)),
                  pl.BlockSpec((2, tn), lambda i, j: (0, j))],
        out_specs=pl.BlockSpec((tt, tn), lambda i, j: (i, j)),
        out_shape=jax.ShapeDtypeStruct((t, n), jnp.float32),
        compiler_params=_params("parallel", "parallel"),
        name=name,
    )(vt, at, res, gates)


def _mm_qk_kernel(a_ref, b_ref, nw_ref, cos_ref, sin_ref, o_ref, *, hd):
    gw = min(MXU_COLS, b_ref.shape[1])
    accs = [jnp.dot(a_ref[...], b_ref[:, g * gw:(g + 1) * gw], preferred_element_type=jnp.float32)
            for g in range(b_ref.shape[1] // gw)]
    cos = cos_ref[...]
    sin = sin_ref[...]
    lane = lax.broadcasted_iota(jnp.int32, (a_ref.shape[0], V7X_LANES), 1)
    per_group = gw // V7X_LANES
    for c in range(b_ref.shape[1] // V7X_LANES):
        sub = c % per_group
        blk = accs[c // per_group][:, sub * V7X_LANES:(sub + 1) * V7X_LANES]
        sq = blk * blk
        if hd == V7X_LANES:
            ms = jnp.sum(sq, axis=-1, keepdims=True) * (1.0 / hd)
        else:
            lo = jnp.sum(jnp.where(lane < hd, sq, 0.0), axis=-1, keepdims=True)
            hi = jnp.sum(jnp.where(lane < hd, 0.0, sq), axis=-1, keepdims=True)
            ms = jnp.where(lane < hd, lo, hi) * (1.0 / hd)
        y = blk * lax.rsqrt(ms + NORM_EPS) * nw_ref[:, c * V7X_LANES:(c + 1) * V7X_LANES]
        if hd == V7X_LANES:
            partner = pltpu.roll(y, hd // 2, 1)
        else:
            partner = jnp.where(lane % hd < hd // 2,
                                pltpu.roll(y, V7X_LANES - hd // 2, 1), pltpu.roll(y, hd // 2, 1))
        o_ref[:, c * V7X_LANES:(c + 1) * V7X_LANES] = (y * cos + partner * sin).astype(o_ref.dtype)


def _mm_qk(a, b, nw, cos, sin, hd, *, name):
    m, k = a.shape
    _, n = b.shape
    tm, tn = _pick(m, ROW_TILES), _pick(n, (512, 256, 128))
    return pl.pallas_call(
        functools.partial(_mm_qk_kernel, hd=hd),
        grid=(m // tm, n // tn),
        in_specs=[pl.BlockSpec((tm, k), lambda i, j: (i, 0)),
                  pl.BlockSpec((k, tn), lambda i, j: (0, j)),
                  pl.BlockSpec((1, tn), lambda i, j: (0, j)),
                  pl.BlockSpec((tm, V7X_LANES), lambda i, j: (i, 0)),
                  pl.BlockSpec((tm, V7X_LANES), lambda i, j: (i, 0))],
        out_specs=pl.BlockSpec((tm, tn), lambda i, j: (i, j)),
        out_shape=jax.ShapeDtypeStruct((m, n), MXU_DTYPE),
        compiler_params=_params("parallel", "parallel"),
        name=name,
    )(a, b, nw, cos, sin)


def _flash_kernel(q_ref, k_ref, v_ref, m0_ref, o_ref, mrun_ref, macc_ref, acc_ref, knorm_ref, *, hd, init_l):
    nchunk = k_ref.shape[1]
    q = q_ref[0]

    @pl.when(pl.program_id(1) == 0)
    def _():
        def body(c, mx):
            kf = k_ref[0, c].astype(jnp.float32)
            return jnp.maximum(mx, jnp.max(jnp.sum(kf * kf, axis=1, keepdims=True), axis=0, keepdims=True))
        k2 = lax.fori_loop(0, nchunk, body, jnp.zeros((1, 1), jnp.float32))
        knorm_ref[...] = jnp.broadcast_to(jnp.sqrt(k2), knorm_ref.shape)

    mrun_ref[...] = m0_ref[0]
    macc_ref[...] = m0_ref[0]
    row = lax.broadcasted_iota(jnp.int32, acc_ref.shape, 0)
    acc_ref[...] = jnp.where(row == hd, init_l, 0.0).astype(jnp.float32)

    def own_max(c, carry):
        s = jnp.dot(k_ref[0, c], q, preferred_element_type=jnp.float32)
        m_new = jnp.maximum(mrun_ref[...], jnp.max(s, axis=0, keepdims=True))
        alpha = jnp.exp2(macc_ref[...] - m_new)
        p = jnp.exp2(s - m_new).astype(MXU_DTYPE)
        acc_ref[...] = acc_ref[...] * alpha + jnp.dot(v_ref[0, c], p, preferred_element_type=jnp.float32)
        mrun_ref[...] = m_new
        macc_ref[...] = m_new
        return carry

    def earlier_max(c, carry):
        m_use = mrun_ref[...]
        s = jnp.dot(k_ref[0, c], q, preferred_element_type=jnp.float32)
        p = jnp.exp2(s - m_use).astype(MXU_DTYPE)
        alpha = jnp.exp2(macc_ref[...] - m_use)
        acc_ref[...] = acc_ref[...] * alpha + jnp.dot(v_ref[0, c], p, preferred_element_type=jnp.float32)
        macc_ref[...] = m_use
        mrun_ref[...] = jnp.maximum(m_use, jnp.max(s, axis=0, keepdims=True))
        return carry

    own_max(0, 0)
    if nchunk > 1:
        qf = q.astype(jnp.float32)
        qnorm = jnp.sqrt(jnp.sum(qf * qf, axis=0, keepdims=True))
        bounded = jnp.max(qnorm * knorm_ref[...] - mrun_ref[...]) <= EXP2_HEADROOM

        @pl.when(bounded)
        def _():
            lax.fori_loop(1, nchunk, earlier_max, 0)

        @pl.when(jnp.logical_not(bounded))
        def _():
            lax.fori_loop(1, nchunk, own_max, 0)

    acc = acc_ref[...]
    o_ref[0] = (acc[:hd] / acc[hd:hd + 1]).astype(o_ref.dtype)


def _flash(qt, kc, vc, m0, hd, m_lanes, init_l, *, name):
    kvh, _, nq = qt.shape
    _, nchunk, tk, _ = kc.shape
    hv = vc.shape[2]
    return pl.pallas_call(
        functools.partial(_flash_kernel, hd=hd, init_l=init_l),
        grid=(kvh, nq // m_lanes),
        in_specs=[pl.BlockSpec((1, hd, m_lanes), lambda h, i: (h, 0, i)),
                  pl.BlockSpec((1, nchunk, tk, hd), lambda h, i: (h, 0, 0, 0)),
                  pl.BlockSpec((1, nchunk, hv, tk), lambda h, i: (h, 0, 0, 0)),
                  pl.BlockSpec((1, 1, m_lanes), lambda h, i: (h, 0, 0))],
        out_specs=pl.BlockSpec((1, hd, m_lanes), lambda h, i: (h, 0, i)),
        out_shape=jax.ShapeDtypeStruct((kvh, hd, nq), MXU_DTYPE),
        scratch_shapes=[pltpu.VMEM((1, m_lanes), jnp.float32),
                        pltpu.VMEM((1, m_lanes), jnp.float32),
                        pltpu.VMEM((hv, m_lanes), jnp.float32),
                        pltpu.VMEM((1, m_lanes), jnp.float32)],
        compiler_params=_params("parallel", "arbitrary"),
        name=name,
    )(qt, kc, vc, m0)


def _window_kernel(q_ref, kc_ref, vc_ref, kp_ref, kb_ref, kn_ref, vp_ref, vb_ref, vn_ref, sink_ref, o_ref,
                   *, hd, tc, seq):
    q = q_ref[0]
    kcat = jnp.concatenate([kc_ref[0], kp_ref[0], kb_ref[0], kn_ref[0]], axis=0)
    vcat = jnp.concatenate([vc_ref[0], vp_ref[0], vb_ref[0], vn_ref[0]], axis=1)
    s = jnp.dot(kcat, q, preferred_element_type=jnp.float32)
    start = pl.program_id(1) * Q_BLOCK
    r = lax.broadcasted_iota(jnp.int32, s.shape, 0)
    lane = lax.broadcasted_iota(jnp.int32, s.shape, 1)
    key_pos = start - WINDOW + (r - tc)
    q_pos = start + lane % Q_BLOCK
    in_band = (jnp.abs(q_pos - key_pos) <= WINDOW) & (key_pos >= 0) & (key_pos < seq)
    s = jnp.where((r < tc) | in_band, s, NEG_INF)
    sink = sink_ref[0]
    m = jnp.maximum(sink, jnp.max(s, axis=0, keepdims=True))
    p = jnp.exp2(s - m).astype(MXU_DTYPE)
    acc = jnp.dot(vcat, p, preferred_element_type=jnp.float32)
    l = acc[hd:hd + 1] + jnp.exp2(sink - m)
    o_ref[0] = (acc[:hd] / l).astype(o_ref.dtype)


def _window(qt, kc, vc, kl, vl, sink_m, hd, m_lanes, tc, seq):
    kvh, _, nq = qt.shape
    hv = vc.shape[1]
    nb = seq // Q_BLOCK
    kspec = lambda f: pl.BlockSpec((1, Q_BLOCK, hd), lambda h, b: (h, f(b), 0))
    vspec = lambda f: pl.BlockSpec((1, hv, Q_BLOCK), lambda h, b: (h, 0, f(b)))
    prev = lambda b: jnp.maximum(b - 1, 0)
    nxt = lambda b: jnp.minimum(b + 1, nb - 1)
    cur = lambda b: b
    return pl.pallas_call(
        functools.partial(_window_kernel, hd=hd, tc=tc, seq=seq),
        grid=(kvh, nb),
        in_specs=[pl.BlockSpec((1, hd, m_lanes), lambda h, b: (h, 0, b)),
                  pl.BlockSpec((1, tc, hd), lambda h, b: (h, 0, 0)),
                  pl.BlockSpec((1, hv, tc), lambda h, b: (h, 0, 0)),
                  kspec(prev), kspec(cur), kspec(nxt), vspec(prev), vspec(cur), vspec(nxt),
                  pl.BlockSpec((1, 1, m_lanes), lambda h, b: (h, 0, 0))],
        out_specs=pl.BlockSpec((1, hd, m_lanes), lambda h, b: (h, 0, b)),
        out_shape=jax.ShapeDtypeStruct((kvh, hd, nq), MXU_DTYPE),
        compiler_params=_params("parallel", "parallel"),
        name="window_attn",
    )(qt, kc, vc, kl, kl, kl, vl, vl, vl, sink_m)


def _rope_tables(seq, tc, pad, hd):
    d_axis = hd // 2
    inv_freq = ROPE_THETA ** (-jnp.arange(0, d_axis, 2, dtype=jnp.float32) / d_axis)
    t = jnp.arange(seq)
    row = (t // GRID_W).astype(jnp.float32)[:, None] * inv_freq
    col = (t % GRID_W).astype(jnp.float32)[:, None] * inv_freq
    ang = jnp.concatenate([row, col], axis=1)
    cos = jnp.concatenate([jnp.cos(ang), jnp.cos(ang)], axis=1)
    sin = jnp.concatenate([-jnp.sin(ang), jnp.sin(ang)], axis=1)
    reps = V7X_LANES // hd
    cos = jnp.tile(cos, (1, reps))
    sin = jnp.tile(sin, (1, reps))
    cos = jnp.concatenate([jnp.ones((tc, V7X_LANES), jnp.float32), cos,
                           jnp.ones((pad, V7X_LANES), jnp.float32)], axis=0)
    sin = jnp.concatenate([jnp.zeros((tc, V7X_LANES), jnp.float32), sin,
                           jnp.zeros((pad, V7X_LANES), jnp.float32)], axis=0)
    return cos, sin


def _head_perm(hd):
    f = hd // 4
    idx = np.arange(hd).reshape(2, 2, f)
    return idx.transpose(1, 0, 2).reshape(hd)


def _to_qt(q, kvh, g, hd, tq):
    n = q.shape[0]
    return q.reshape(n // tq, tq, kvh, g, hd).transpose(2, 4, 0, 3, 1).reshape(kvh, hd, n * g)


def _from_ot(ot, kvh, g, hd, tq):
    n = ot.shape[2] // g
    return ot.reshape(kvh, hd, n // tq, g, tq).transpose(2, 4, 0, 3, 1).reshape(n, kvh * g * hd)


def _v_aug(v, kvh, hd):
    n = v.shape[0]
    vt = v.reshape(n, kvh, hd).transpose(1, 2, 0)
    ones = jnp.ones((kvh, 1, n), v.dtype)
    zeros = jnp.zeros((kvh, V7X_BF16_ROWS - 1, n), v.dtype)
    return jnp.concatenate([vt, ones, zeros], axis=1)


def _chunk_keys(k, vt, tk):
    kvh, n, hd = k.shape
    kc = k.reshape(kvh, n // tk, tk, hd)
    vc = vt.reshape(kvh, vt.shape[1], n // tk, tk).transpose(0, 2, 1, 3)
    return kc, vc


def _attention_layer(h, w_qkv, w_o, q_norm, k_norm, sink, hd, g, tc, seq, windowed, with_ctx_out):
    t_pad, d = h.shape
    t = tc + seq
    nheads = d // hd
    kvh = nheads // g
    qd, kd = nheads * hd, kvh * hd
    perm = _head_perm(hd)
    wq = w_qkv[:, :qd].reshape(d, nheads, hd)[:, :, perm].reshape(d, qd).astype(MXU_DTYPE)
    wk = w_qkv[:, qd:qd + kd].reshape(d, kvh, hd)[:, :, perm].reshape(d, kd).astype(MXU_DTYPE)
    wv = w_qkv[:, qd + kd:].astype(MXU_DTYPE)
    qscale = (hd ** -0.5) * LOG2E
    nwq = jnp.tile(q_norm[perm] * qscale, nheads).reshape(1, qd)
    nwk = jnp.tile(k_norm[perm], kvh).reshape(1, kd)
    cos, sin = _rope_tables(seq, tc, t_pad - t, hd)
    q = _mm_qk(h, wq, nwq, cos, sin, hd, name="proj_q")
    k = _mm_qk(h, wk, nwk, cos, sin, hd, name="proj_k")[:t]
    v = _mm(h, wv, MXU_DTYPE, name="proj_v")[:t]

    m_lanes = 1024
    tq = m_lanes // g
    assert not windowed or tq == Q_BLOCK
    kk = k.reshape(t, kvh, hd).transpose(1, 0, 2)
    vt = _v_aug(v, kvh, hd)
    qt_lat = _to_qt(q[tc:t], kvh, g, hd, tq)
    if sink is None:
        m_init = jnp.full((kvh, 1, m_lanes), NEG_INF, jnp.float32)
        init_l = 0.0
    else:
        m_init = jnp.repeat(sink.reshape(kvh, g).astype(jnp.float32) * LOG2E, tq, axis=1).reshape(kvh, 1, m_lanes)
        init_l = 1.0
    if windowed:
        o_lat = _window(qt_lat, kk[:, :tc], vt[:, :, :tc], kk[:, tc:], vt[:, :, tc:], m_init, hd, m_lanes, tc, seq)
    else:
        tk = _pick(t, (1280, 1024, 512, 256, 128))
        kc, vc = _chunk_keys(kk, vt, tk)
        o_lat = _flash(qt_lat, kc, vc, m_init, hd, m_lanes, init_l, name="global_attn")
    o_lat = _from_ot(o_lat, kvh, g, hd, tq)
    if with_ctx_out:
        kc, vc = _chunk_keys(kk[:, :tc], vt[:, :, :tc], tc)
        o_ctx = _flash(_to_qt(q[:tc], kvh, g, hd, tq), kc, vc, m_init, hd, m_lanes, init_l, name="ctx_attn")
        o_ctx = _from_ot(o_ctx, kvh, g, hd, tq)
    else:
        o_ctx = jnp.zeros((tc, qd), MXU_DTYPE)
    o_pad = jnp.zeros((t_pad - t, qd), MXU_DTYPE)
    return jnp.concatenate([o_ctx, o_lat, o_pad], axis=0), w_o.astype(MXU_DTYPE)


def _dft_parts(n, scale):
    k = np.arange(n)
    ang = 2.0 * np.pi * ((k[:, None] * k[None, :]) % n) / n
    return np.cos(ang) * scale, -np.sin(ang) * scale


def _chan_dft_kernel(x_ref, c_ref, s_ref, o_ref):
    x = x_ref[...]
    o_ref[0] = jnp.dot(x, c_ref[...], preferred_element_type=jnp.float32).astype(o_ref.dtype)
    o_ref[1] = jnp.dot(x, s_ref[...], preferred_element_type=jnp.float32).astype(o_ref.dtype)


def _chan_dft(h, gc):
    t, d = h.shape
    re, im = _dft_parts(gc, gc ** -0.5)
    tm = _pick(t, ROW_TILES)
    return pl.pallas_call(
        _chan_dft_kernel,
        grid=(t // tm, d // gc),
        in_specs=[pl.BlockSpec((tm, gc), lambda i, j: (i, j)),
                  pl.BlockSpec((gc, gc), lambda i, j: (0, 0)),
                  pl.BlockSpec((gc, gc), lambda i, j: (0, 0))],
        out_specs=pl.BlockSpec((2, tm, gc), lambda i, j: (0, i, j)),
        out_shape=jax.ShapeDtypeStruct((2, t, d), MXU_DTYPE),
        compiler_params=_params("parallel", "parallel"),
        name="chan_dft",
    )(h, jnp.asarray(re, MXU_DTYPE), jnp.asarray(im, MXU_DTYPE))


def _pos_stage1_kernel(m_ref, y_ref, tr_ref, ti_ref, o_ref, *, n1):
    y = jnp.concatenate([y_ref[0], y_ref[1]], axis=0)
    a = jnp.dot(m_ref[...], y, preferred_element_type=jnp.float32)
    ar, ai = a[:n1], a[n1:]
    tr, ti = tr_ref[0][:, 0:1], ti_ref[0][:, 0:1]
    o_ref[0] = (ar * tr - ai * ti).astype(o_ref.dtype)
    o_ref[1] = (ar * ti + ai * tr).astype(o_ref.dtype)


def _pos_stage2_kernel(m_ref, b_ref, o_ref):
    b = jnp.concatenate([b_ref[0, 0], b_ref[1, 0]], axis=0)
    o_ref[0] = jnp.dot(m_ref[...], b, preferred_element_type=jnp.float32).astype(o_ref.dtype)


def _pos_dft_two_stage(y, n1, n2):
    _, s, d = y.shape
    sc = float(s) ** -0.25
    r1, i1 = _dft_parts(n1, sc)
    m1 = np.block([[r1, -i1], [i1, r1]])
    f1 = np.arange(n1)[None, :]
    t2 = np.arange(n2)[:, None]
    ang = 2.0 * np.pi * (t2 * f1) / s
    tw_r = np.broadcast_to(np.cos(ang)[:, :, None], (n2, n1, V7X_LANES))
    tw_i = np.broadcast_to(-np.sin(ang)[:, :, None], (n2, n1, V7X_LANES))
    y4 = y.reshape(2, n1, n2 * d)
    b = pl.pallas_call(
        functools.partial(_pos_stage1_kernel, n1=n1),
        grid=(n2,),
        in_specs=[pl.BlockSpec((2 * n1, 2 * n1), lambda j: (0, 0)),
                  pl.BlockSpec((2, n1, d), lambda j: (0, 0, j)),
                  pl.BlockSpec((1, n1, V7X_LANES), lambda j: (j, 0, 0)),
                  pl.BlockSpec((1, n1, V7X_LANES), lambda j: (j, 0, 0))],
        out_specs=pl.BlockSpec((2, n1, d), lambda j: (0, 0, j)),
        out_shape=jax.ShapeDtypeStruct((2, n1, n2 * d), MXU_DTYPE),
        compiler_params=_params("parallel"),
        name="pos_dft_stage1",
    )(jnp.asarray(m1, MXU_DTYPE), y4, jnp.asarray(tw_r, jnp.float32), jnp.asarray(tw_i, jnp.float32))
    b4 = b.reshape(2, n1, n2, d)
    r2, i2 = _dft_parts(n2, sc)
    m2 = np.concatenate([r2, -i2], axis=1)
    tcn = _pick(d, (2048, 1024, 512, 256, 128))
    out = pl.pallas_call(
        _pos_stage2_kernel,
        grid=(n1, d // tcn),
        in_specs=[pl.BlockSpec((n2, 2 * n2), lambda f, j: (0, 0)),
                  pl.BlockSpec((2, 1, n2, tcn), lambda f, j: (0, f, 0, j))],
        out_specs=pl.BlockSpec((1, n2, tcn), lambda f, j: (f, 0, j)),
        out_shape=jax.ShapeDtypeStruct((n1, n2, d), MXU_DTYPE),
        compiler_params=_params("parallel", "parallel"),
        name="pos_dft_stage2",
    )(jnp.asarray(m2, MXU_DTYPE), b4)
    return out.transpose(1, 0, 2).reshape(s, d)


def _fourier_mix(h, tc, seq):
    t_pad, d = h.shape
    y = _chan_dft(h, d // FNET_GROUPS)
    rc, ic = _dft_parts(tc, tc ** -0.5)
    mc = np.concatenate([rc, -ic], axis=1)
    yc = jnp.concatenate([y[0, :tc], y[1, :tc]], axis=0)
    mixed_c = _mm(jnp.asarray(mc, MXU_DTYPE), yc, MXU_DTYPE, name="ctx_pos_dft")
    n1 = 1 << (int(math.log2(seq)) // 2)
    n2 = seq // n1
    mixed_l = _pos_dft_two_stage(y[:, tc:tc + seq], n1, n2)
    return jnp.concatenate([mixed_c, mixed_l, jnp.zeros((t_pad - tc - seq, d), MXU_DTYPE)], axis=0)


def _top16(s):
    n = s.shape[0]
    rows = lax.broadcasted_iota(jnp.int32, s.shape, 0)
    row16 = lax.broadcasted_iota(jnp.int32, (PEER_TOPK, s.shape[1]), 0)
    rank = jnp.full(s.shape, RANK_NONE, jnp.float32)
    sv = jnp.zeros((PEER_TOPK, s.shape[1]), jnp.float32)
    v = s
    for k in range(PEER_TOPK):
        m = jnp.max(v, axis=0, keepdims=True)
        idx = jnp.min(jnp.where(v == m, rows, n), axis=0, keepdims=True)
        hit = rows == idx
        rank = jnp.where(hit, float(k), rank)
        v = jnp.where(hit, -jnp.inf, v)
        sv = jnp.where(row16 == k, m, sv)
    return rank, sv


_CAND_GROUPS = ((0, 0, 8), (0, 8, 8), (1, 0, 8), (2, 0, 5), (3, 0, 4), (4, 0, 3), (5, 0, 2), (6, 0, 2), (7, 0, 2))


def _select_pairs(sv0, sv1):
    lanes = sv0.shape[1]
    sub = lax.broadcasted_iota(jnp.int32, (V7X_SUBLANES, lanes), 0)
    cands, poss = [], []
    for r0, r1_0, cnt in _CAND_GROUPS:
        c = sv0[r0:r0 + 1] + sv1[r1_0:r1_0 + V7X_SUBLANES]
        cands.append(jnp.where(sub < cnt, c, -jnp.inf))
        poss.append(r0 * PEER_TOPK + r1_0 + sub)
    cands.append(sv0[V7X_SUBLANES:] + sv1[0:1])
    poss.append((V7X_SUBLANES + sub) * PEER_TOPK)
    cand = jnp.concatenate(cands, axis=0)
    pos = jnp.concatenate(poss, axis=0)
    cmax = sv0[0:1] + sv1[0:1]
    big = PEER_TOPK * PEER_TOPK
    c = cand
    sel = jnp.zeros(cand.shape, jnp.float32)
    for _ in range(PEER_TOPK):
        m = jnp.max(c, axis=0, keepdims=True)
        pidx = jnp.min(jnp.where(c == m, pos, big), axis=0, keepdims=True)
        hit = pos == pidx
        sel = jnp.where(hit, 1.0, sel)
        c = jnp.where(hit, -jnp.inf, c)
    z = jnp.sum(jnp.where(sel > 0.0, jnp.exp(cand - cmax), 0.0), axis=0, keepdims=True)
    g = V7X_SUBLANES
    rows = [jnp.sum(sel[0:2 * g], axis=0, keepdims=True)]
    for i in range(2, len(_CAND_GROUPS)):
        rows.append(jnp.sum(sel[i * g:(i + 1) * g], axis=0, keepdims=True))
    rows.append(sel[len(_CAND_GROUPS) * g:])
    return jnp.concatenate(rows, axis=0), z


def _peer_select_kernel(q_ref, keys_ref, r1_ref, b_ref, cnt_ref, a_ref):
    def head(hh, carry):
        base = pl.multiple_of(hh * 2 * PEER_HALF, 2 * PEER_HALF)
        q0 = q_ref[pl.ds(base, PEER_HALF), :]
        q1 = q_ref[pl.ds(base + PEER_HALF, PEER_HALF), :]
        s0 = jnp.dot(keys_ref[hh, 0], q0, preferred_element_type=jnp.float32, precision=lax.Precision.HIGHEST)
        s1 = jnp.dot(keys_ref[hh, 1], q1, preferred_element_type=jnp.float32, precision=lax.Precision.HIGHEST)
        rank0, sv0 = _top16(s0)
        rank1, sv1 = _top16(s1)
        cnt16, z = _select_pairs(sv0, sv1)
        cnt = jnp.zeros(s0.shape, jnp.float32)
        for r in range(PEER_TOPK):
            cnt = jnp.where(rank0 == float(r), cnt16[r:r + 1], cnt)
        r1_ref[hh] = rank1
        b_ref[hh] = jnp.exp(s1 - sv1[0:1])
        cnt_ref[hh] = cnt
        a_ref[hh] = jnp.exp(s0 - sv0[0:1]) / z
        return carry

    lax.fori_loop(0, PEER_HEADS, head, 0)


def _peer_select(qt, sub_keys):
    _, t = qt.shape
    tl = _pick(t, (2 * V7X_LANES, V7X_LANES))
    shp = jax.ShapeDtypeStruct((PEER_HEADS, PEER_N_KEYS, t), jnp.float32)
    ospec = pl.BlockSpec((PEER_HEADS, PEER_N_KEYS, tl), lambda i: (0, 0, i))
    return pl.pallas_call(
        _peer_select_kernel,
        grid=(t // tl,),
        in_specs=[pl.BlockSpec((qt.shape[0], tl), lambda i: (0, i)),
                  pl.BlockSpec(sub_keys.shape, lambda i: (0, 0, 0, 0))],
        out_specs=[ospec] * 4,
        out_shape=[shp] * 4,
        compiler_params=_params("parallel"),
        name="peer_select",
    )(qt, sub_keys)


def _gelu_exact(x):
    return 0.5 * x * (1.0 + lax.erf(x * (2.0 ** -0.5)))


def _peer_act_kernel(u_ref, h_ref, r1_ref, b_ref, cnt_ref, a_ref, o_ref, *, iblocks):
    nk = PEER_N_KEYS
    groups = [slice(g * MXU_COLS, (g + 1) * MXU_COLS) for g in range(h_ref.shape[1] // MXU_COLS)]
    pres = [jnp.dot(u_ref[...], h_ref[:, cols], preferred_element_type=jnp.float32) for cols in groups]
    for pre, cols in zip(pres, groups):
        for ii in range(iblocks):
            w = jnp.zeros((nk, MXU_COLS), jnp.float32)
            for hh in range(PEER_HEADS):
                sel = r1_ref[hh, :, cols] < cnt_ref[hh, ii:ii + 1, cols]
                w = w + jnp.where(sel, b_ref[hh, :, cols], 0.0) * a_ref[hh, ii:ii + 1, cols]
            o_ref[ii * nk:(ii + 1) * nk, cols] = (_gelu_exact(pre[ii * nk:(ii + 1) * nk]) * w).astype(o_ref.dtype)


def _peer_act(u, ht, r1, b, cnt, a):
    e, d = u.shape
    _, t = ht.shape
    iblocks = V7X_SUBLANES
    te = iblocks * PEER_N_KEYS
    tt = _pick(t, TOKEN_TILES)
    full = pl.BlockSpec((PEER_HEADS, PEER_N_KEYS, tt), lambda i, j: (0, 0, i))
    part = pl.BlockSpec((PEER_HEADS, iblocks, tt), lambda i, j: (0, j, i))
    return pl.pallas_call(
        functools.partial(_peer_act_kernel, iblocks=iblocks),
        grid=(t // tt, e // te),
        in_specs=[pl.BlockSpec((te, d), lambda i, j: (j, 0)),
                  pl.BlockSpec((d, tt), lambda i, j: (0, i)),
                  full, full, part, part],
        out_specs=pl.BlockSpec((te, tt), lambda i, j: (j, i)),
        out_shape=jax.ShapeDtypeStruct((e, t), MXU_DTYPE),
        compiler_params=_params("parallel", "parallel"),
        name="peer_act",
    )(u, ht, r1, b, cnt, a)


def _peer(x, nw, mod_ctx, mod_lat, gates, tc, w_q, sub_keys, u_tab, v_tab):
    ht = _modulate(x, nw, mod_ctx, mod_lat, tc, transpose=True)
    qt = _mm(w_q.T.astype(MXU_DTYPE), ht, jnp.float32, tm_prefs=(512, 256, 128),
             tn_prefs=TOKEN_TILES, name="peer_query")
    r1, b, cnt, a = _peer_select(qt, sub_keys)
    act = _peer_act(u_tab.astype(MXU_DTYPE), ht, r1, b, cnt, a)
    return _mm_res_t(v_tab.T.astype(MXU_DTYPE), act, x, gates, tc, name="peer_out")


def kernel(x, c, ctx, c_ctx, ada_w, ada_b, mix_norm, ffn_norm, a_w_qkv, a_w_o, a_q_norm, a_k_norm,
           b_w_qkv, b_w_o, b_q_norm, b_k_norm, b_sink, f_w_out, peer_w_q, peer_sub_keys, peer_u, peer_v):
    batch, seq, d = x.shape
    assert batch == 1 and ctx.shape[0] == 1
    tc = ctx.shape[1]
    depth = ada_w.shape[0]
    pad = -(tc + seq) % TOKEN_TILES[0]
    xs = jnp.concatenate([ctx[0], x[0], jnp.zeros((pad, d), x.dtype)], axis=0)
    mods = _ada(jnp.stack([c[0], c_ctx]), ada_w, ada_b)
    mods = mods.reshape(depth, 2, N_MOD, d)
    for layer in range(depth):
        last = layer == depth - 1
        kind, j = layer % N_MIXERS, layer // N_MIXERS
        lat, cx = mods[layer, 0], mods[layer, 1]
        h = _modulate(xs, mix_norm[layer], cx[0:2], lat[0:2], tc, transpose=False)
        if kind == 0:
            y, w_out = _attention_layer(h, a_w_qkv[j], a_w_o[j], a_q_norm[j], a_k_norm[j], None,
                                        A_HEAD_DIM, A_GROUP, tc, seq, False, not last)
        elif kind == 1:
            y, w_out = _attention_layer(h, b_w_qkv[j], b_w_o[j], b_q_norm[j], b_k_norm[j], b_sink[j],
                                        B_HEAD_DIM, B_GROUP, tc, seq, True, not last)
        else:
            y, w_out = _fourier_mix(h, tc, seq), f_w_out[j].astype(MXU_DTYPE)
        xs = _mm_res(y, w_out, xs, jnp.stack([cx[2], lat[2]]), tc, name="mixer_out")
        xs = _peer(xs, ffn_norm[layer], cx[3:5], lat[3:5], jnp.stack([cx[5], lat[5]]), tc,
                   peer_w_q[layer], peer_sub_keys[layer], peer_u[layer], peer_v[layer])
    return xs[tc:tc + seq][None]
```

```python
import functools
import math

import numpy as np
import jax
import jax.numpy as jnp
from jax import lax
from jax.experimental import pallas as pl
from jax.experimental.pallas import tpu as pltpu

GRID_W = 64
Q_BLOCK = 128
ROPE_THETA = 10000.0
NORM_EPS = 1e-6
NEG_INF = -1e30
N_MOD = 6
N_MIXERS = 3
A_HEAD_DIM = 128
A_GROUP = 4
B_HEAD_DIM = 64
B_GROUP = 8
WINDOW = 128
FNET_GROUPS = 8
PEER_HEADS = 8
PEER_N_KEYS = 96
PEER_TOPK = 16
PEER_HALF = 128

V7X_LANES = 128
V7X_SUBLANES = 8
V7X_BF16_ROWS = 16
MXU_COLS = 256
V7X_VMEM_BYTES = 64 * 1024 * 1024
VMEM_LIMIT = V7X_VMEM_BYTES - 8 * 1024 * 1024

MXU_DTYPE = jnp.bfloat16
LOG2E = 1.4426950408889634
RANK_NONE = 99.0
ROW_TILES = (1536, 768, 512, 256, 128)
TOKEN_TILES = (768, 512, 256)
EXP2_HEADROOM = 64.0


def _params(*sem):
    return pltpu.CompilerParams(dimension_semantics=sem, vmem_limit_bytes=VMEM_LIMIT)


def _pick(n, prefs):
    for p in prefs:
        if n % p == 0:
            return p
    return n


def _ada_kernel(cond_ref, w_ref, b_ref, o_ref):
    w = w_ref[0]
    reps = w.shape[1] // V7X_LANES
    rows = []
    for r in range(2):
        c = cond_ref[r]
        s = c * jax.nn.sigmoid(c)
        rows.append(jnp.sum(w * jnp.tile(s, (1, reps)), axis=0, keepdims=True))
    o_ref[0] = jnp.concatenate(rows, axis=0) + b_ref[0]


def _ada(cond2, ada_w, ada_b):
    depth, d, n = ada_w.shape
    tn = _pick(n, (512, 256, 128))
    cond_rep = jnp.broadcast_to(cond2[:, :, None], (2, d, V7X_LANES))
    return pl.pallas_call(
        _ada_kernel,
        grid=(depth, n // tn),
        in_specs=[pl.BlockSpec((2, d, V7X_LANES), lambda l, j: (0, 0, 0)),
                  pl.BlockSpec((1, d, tn), lambda l, j: (l, 0, j)),
                  pl.BlockSpec((1, 1, tn), lambda l, j: (l, 0, j))],
        out_specs=pl.BlockSpec((1, 2, tn), lambda l, j: (l, 0, j)),
        out_shape=jax.ShapeDtypeStruct((depth, 2, n), jnp.float32),
        compiler_params=_params("parallel", "parallel"),
        name="ada",
    )(cond_rep, ada_w, ada_b.reshape(depth, 1, n))


def _modulate_kernel(x_ref, nw_ref, mod_ref, *o_refs, layouts):
    x = x_ref[...]
    y = x * lax.rsqrt(jnp.mean(x * x, axis=-1, keepdims=True) + NORM_EPS) * nw_ref[...]
    shift = mod_ref[0, 0:1, :]
    scale = mod_ref[0, 1:2, :]
    h = y * (1.0 + scale) + shift
    for o_ref, layout in zip(o_refs, layouts):
        o_ref[...] = (h.T if layout == "feature_major" else h).astype(o_ref.dtype)


def _modulate(x, nw, mod_ctx, mod_lat, tc, layouts):
    t, d = x.shape
    tm = _pick(math.gcd(t, tc), (256, 128))
    ctx_tiles = tc // tm
    mods = jnp.stack([mod_ctx, mod_lat])
    shapes = [jax.ShapeDtypeStruct((d, t) if lay == "feature_major" else (t, d), MXU_DTYPE) for lay in layouts]
    specs = [pl.BlockSpec((d, tm), lambda i: (0, i)) if lay == "feature_major"
             else pl.BlockSpec((tm, d), lambda i: (i, 0)) for lay in layouts]
    return pl.pallas_call(
        functools.partial(_modulate_kernel, layouts=tuple(layouts)),
        grid=(t // tm,),
        in_specs=[pl.BlockSpec((tm, d), lambda i: (i, 0)),
                  pl.BlockSpec((1, d), lambda i: (0, 0)),
                  pl.BlockSpec((1, 2, d), lambda i: (jnp.where(i < ctx_tiles, 0, 1), 0, 0))],
        out_specs=specs,
        out_shape=shapes,
        compiler_params=_params("parallel"),
        name="modulate",
    )(x, nw.reshape(1, d), mods)


def _mm_kernel(a_ref, b_ref, o_ref):
    o_ref[...] = jnp.dot(a_ref[...], b_ref[...], preferred_element_type=jnp.float32).astype(o_ref.dtype)


def _mm(a, b, out_dtype, *, tm_prefs=ROW_TILES, tn_prefs=(512, 256, 128), name="mm"):
    m, k = a.shape
    _, n = b.shape
    tm, tn = _pick(m, tm_prefs), _pick(n, tn_prefs)
    return pl.pallas_call(
        _mm_kernel,
        grid=(m // tm, n // tn),
        in_specs=[pl.BlockSpec((tm, k), lambda i, j: (i, 0)),
                  pl.BlockSpec((k, tn), lambda i, j: (0, j))],
        out_specs=pl.BlockSpec((tm, tn), lambda i, j: (i, j)),
        out_shape=jax.ShapeDtypeStruct((m, n), out_dtype),
        compiler_params=_params("parallel", "parallel"),
        name=name,
    )(a, b)


def _mm_res_kernel(a_ref, b_ref, res_ref, g_ref, o_ref, *, tc, transposed):
    acc = jnp.dot(a_ref[...], b_ref[...], preferred_element_type=jnp.float32)
    if transposed:
        acc = acc.T
        row0 = pl.program_id(0) * acc.shape[0]
    else:
        row0 = pl.program_id(0) * acc.shape[0]
    rid = row0 + lax.broadcasted_iota(jnp.int32, acc.shape, 0)
    gate = jnp.where(rid < tc, g_ref[0:1, :], g_ref[1:2, :])
    o_ref[...] = res_ref[...] + gate * acc


def _mm_res(a, b, res, gates, tc, *, name="mm_res"):
    m, k = a.shape
    _, n = b.shape
    tm, tn = _pick(m, ROW_TILES), _pick(n, (512, 256, 128))
    return pl.pallas_call(
        functools.partial(_mm_res_kernel, tc=tc, transposed=False),
        grid=(m // tm, n // tn),
        in_specs=[pl.BlockSpec((tm, k), lambda i, j: (i, 0)),
                  pl.BlockSpec((k, tn), lambda i, j: (0, j)),
                  pl.BlockSpec((tm, tn), lambda i, j: (i, j)),
                  pl.BlockSpec((2, tn), lambda i, j: (0, j))],
        out_specs=pl.BlockSpec((tm, tn), lambda i, j: (i, j)),
        out_shape=jax.ShapeDtypeStruct((m, n), jnp.float32),
        compiler_params=_params("parallel", "parallel"),
        name=name,
    )(a, b, res, gates)


def _mm_res_t(vt, at, res, gates, tc, *, name="mm_res_t"):
    n, k = vt.shape
    _, t = at.shape
    tt, tn = _pick(t, TOKEN_TILES), _pick(n, (512, 256, 128) if k <= 4096 else (256, 128))
    return pl.pallas_call(
        functools.partial(_mm_res_kernel, tc=tc, transposed=True),
        grid=(t // tt, n // tn),
        in_specs=[pl.BlockSpec((tn, k), lambda i, j: (j, 0)),
                  pl.BlockSpec((k, tt), lambda i, j: (0, i)),
                  pl.BlockSpec((tt, tn), lambda i, j: (i, j)),
                  pl.BlockSpec((2, tn), lambda i, j: (0, j))],
        out_specs=pl.BlockSpec((tt, tn), lambda i, j: (i, j)),
        out_shape=jax.ShapeDtypeStruct((t, n), jnp.float32),
        compiler_params=_params("parallel", "parallel"),
        name=name,
    )(vt, at, res, gates)


def _mm_qk_kernel(a_ref, b_ref, nw_ref, cos_ref, sin_ref, o_ref, *, hd):
    gw = min(MXU_COLS, b_ref.shape[1])
    accs = [jnp.dot(a_ref[...], b_ref[:, g * gw:(g + 1) * gw], preferred_element_type=jnp.float32)
            for g in range(b_ref.shape[1] // gw)]
    cos = cos_ref[...]
    sin = sin_ref[...]
    lane = lax.broadcasted_iota(jnp.int32, (a_ref.shape[0], V7X_LANES), 1)
    per_group = gw // V7X_LANES
    for c in range(b_ref.shape[1] // V7X_LANES):
        sub = c % per_group
        blk = accs[c // per_group][:, sub * V7X_LANES:(sub + 1) * V7X_LANES]
        sq = blk * blk
        if hd == V7X_LANES:
            ms = jnp.sum(sq, axis=-1, keepdims=True) * (1.0 / hd)
        else:
            lo = jnp.sum(jnp.where(lane < hd, sq, 0.0), axis=-1, keepdims=True)
            hi = jnp.sum(jnp.where(lane < hd, 0.0, sq), axis=-1, keepdims=True)
            ms = jnp.where(lane < hd, lo, hi) * (1.0 / hd)
        y = blk * lax.rsqrt(ms + NORM_EPS) * nw_ref[:, c * V7X_LANES:(c + 1) * V7X_LANES]
        if hd == V7X_LANES:
            partner = pltpu.roll(y, hd // 2, 1)
        else:
            partner = jnp.where(lane % hd < hd // 2,
                                pltpu.roll(y, V7X_LANES - hd // 2, 1), pltpu.roll(y, hd // 2, 1))
        o_ref[:, c * V7X_LANES:(c + 1) * V7X_LANES] = (y * cos + partner * sin).astype(o_ref.dtype)


def _mm_qk(a, b, nw, cos, sin, hd, *, name):
    m, k = a.shape
    _, n = b.shape
    tm, tn = _pick(m, ROW_TILES), _pick(n, (512, 256, 128))
    return pl.pallas_call(
        functools.partial(_mm_qk_kernel, hd=hd),
        grid=(m // tm, n // tn),
        in_specs=[pl.BlockSpec((tm, k), lambda i, j: (i, 0)),
                  pl.BlockSpec((k, tn), lambda i, j: (0, j)),
                  pl.BlockSpec((1, tn), lambda i, j: (0, j)),
                  pl.BlockSpec((tm, V7X_LANES), lambda i, j: (i, 0)),
                  pl.BlockSpec((tm, V7X_LANES), lambda i, j: (i, 0))],
        out_specs=pl.BlockSpec((tm, tn), lambda i, j: (i, j)),
        out_shape=jax.ShapeDtypeStruct((m, n), MXU_DTYPE),
        compiler_params=_params("parallel", "parallel"),
        name=name,
    )(a, b, nw, cos, sin)


def _mm_qk_t_kernel(w_ref, h_ref, nw_ref, cos_ref, sin_ref, o_ref, *, hd):
    acc = jnp.dot(w_ref[...], h_ref[...], preferred_element_type=jnp.float32)
    cos = cos_ref[...]
    sin = sin_ref[...]
    reps = acc.shape[1] // V7X_LANES
    half = hd // 2
    for r in range(acc.shape[0] // hd):
        x = acc[r * hd:(r + 1) * hd]
        ms = jnp.sum(x * x, axis=0, keepdims=True) * (1.0 / hd)
        y = x * lax.rsqrt(ms + NORM_EPS) * jnp.tile(nw_ref[r * hd:(r + 1) * hd], (1, reps))
        y1, y2 = y[:half], y[half:]
        o_ref[r * hd:r * hd + half] = (y1 * cos - y2 * sin).astype(o_ref.dtype)
        o_ref[r * hd + half:(r + 1) * hd] = (y1 * sin + y2 * cos).astype(o_ref.dtype)


def _mm_qk_t(wt, ht, nw, cos, sin, hd, *, name):
    n, k = wt.shape
    _, t = ht.shape
    tf, tt = _pick(n, (512, 256, 128)), _pick(t, TOKEN_TILES)
    nw_rep = jnp.broadcast_to(nw[:, None], (n, V7X_LANES))
    return pl.pallas_call(
        functools.partial(_mm_qk_t_kernel, hd=hd),
        grid=(t // tt, n // tf),
        in_specs=[pl.BlockSpec((tf, k), lambda i, j: (j, 0)),
                  pl.BlockSpec((k, tt), lambda i, j: (0, i)),
                  pl.BlockSpec((tf, V7X_LANES), lambda i, j: (j, 0)),
                  pl.BlockSpec((hd // 2, tt), lambda i, j: (0, i)),
                  pl.BlockSpec((hd // 2, tt), lambda i, j: (0, i))],
        out_specs=pl.BlockSpec((tf, tt), lambda i, j: (j, i)),
        out_shape=jax.ShapeDtypeStruct((n, t), MXU_DTYPE),
        compiler_params=_params("parallel", "parallel"),
        name=name,
    )(wt, ht, nw_rep, cos, sin)


def _flash_kernel(q_ref, k_ref, v_ref, m0_ref, o_ref, mrun_ref, macc_ref, acc_ref, knorm_ref, *, hd, init_l):
    nchunk = k_ref.shape[1]
    q = jnp.concatenate([q_ref[0, g] for g in range(q_ref.shape[1])], axis=1)

    @pl.when(pl.program_id(1) == 0)
    def _():
        def body(c, mx):
            kf = k_ref[0, c].astype(jnp.float32)
            return jnp.maximum(mx, jnp.max(jnp.sum(kf * kf, axis=1, keepdims=True), axis=0, keepdims=True))
        k2 = lax.fori_loop(0, nchunk, body, jnp.zeros((1, 1), jnp.float32))
        knorm_ref[...] = jnp.broadcast_to(jnp.sqrt(k2), knorm_ref.shape)

    mrun_ref[...] = m0_ref[0]
    macc_ref[...] = m0_ref[0]
    row = lax.broadcasted_iota(jnp.int32, acc_ref.shape, 0)
    acc_ref[...] = jnp.where(row == hd, init_l, 0.0).astype(jnp.float32)

    def own_max(c, carry):
        s = jnp.dot(k_ref[0, c], q, preferred_element_type=jnp.float32)
        m_new = jnp.maximum(mrun_ref[...], jnp.max(s, axis=0, keepdims=True))
        alpha = jnp.exp2(macc_ref[...] - m_new)
        p = jnp.exp2(s - m_new).astype(MXU_DTYPE)
        acc_ref[...] = acc_ref[...] * alpha + jnp.dot(v_ref[0, c], p, preferred_element_type=jnp.float32)
        mrun_ref[...] = m_new
        macc_ref[...] = m_new
        return carry

    def earlier_max(c, carry):
        m_use = mrun_ref[...]
        s = jnp.dot(k_ref[0, c], q, preferred_element_type=jnp.float32)
        p = jnp.exp2(s - m_use).astype(MXU_DTYPE)
        alpha = jnp.exp2(macc_ref[...] - m_use)
        acc_ref[...] = acc_ref[...] * alpha + jnp.dot(v_ref[0, c], p, preferred_element_type=jnp.float32)
        macc_ref[...] = m_use
        mrun_ref[...] = jnp.maximum(m_use, jnp.max(s, axis=0, keepdims=True))
        return carry

    own_max(0, 0)
    if nchunk > 1:
        qf = q.astype(jnp.float32)
        qnorm = jnp.sqrt(jnp.sum(qf * qf, axis=0, keepdims=True))
        bounded = jnp.max(qnorm * knorm_ref[...] - mrun_ref[...]) <= EXP2_HEADROOM

        @pl.when(bounded)
        def _():
            lax.fori_loop(1, nchunk, earlier_max, 0)

        @pl.when(jnp.logical_not(bounded))
        def _():
            lax.fori_loop(1, nchunk, own_max, 0)

    acc = acc_ref[...]
    o = (acc[:hd] / acc[hd:hd + 1]).astype(o_ref.dtype)
    tq = o_ref.shape[3]
    for g in range(o_ref.shape[1]):
        o_ref[0, g] = o[:, g * tq:(g + 1) * tq]


def _flash(qt, kc, vc, m0, init_l, first_tile, n_tiles, *, name):
    kvh, g, hd, _ = qt.shape
    _, nchunk, tk, _ = kc.shape
    hv = vc.shape[2]
    m_lanes = m0.shape[2]
    tq = m_lanes // g
    return pl.pallas_call(
        functools.partial(_flash_kernel, hd=hd, init_l=init_l),
        grid=(kvh, n_tiles),
        in_specs=[pl.BlockSpec((1, g, hd, tq), lambda h, i: (h, 0, 0, i + first_tile)),
                  pl.BlockSpec((1, nchunk, tk, hd), lambda h, i: (h, 0, 0, 0)),
                  pl.BlockSpec((1, nchunk, hv, tk), lambda h, i: (h, 0, 0, 0)),
                  pl.BlockSpec((1, 1, m_lanes), lambda h, i: (h, 0, 0))],
        out_specs=pl.BlockSpec((1, g, hd, tq), lambda h, i: (h, 0, 0, i)),
        out_shape=jax.ShapeDtypeStruct((kvh, g, hd, n_tiles * tq), MXU_DTYPE),
        scratch_shapes=[pltpu.VMEM((1, m_lanes), jnp.float32),
                        pltpu.VMEM((1, m_lanes), jnp.float32),
                        pltpu.VMEM((hv, m_lanes), jnp.float32),
                        pltpu.VMEM((1, m_lanes), jnp.float32)],
        compiler_params=_params("parallel", "arbitrary"),
        name=name,
    )(qt, kc, vc, m0)


def _window_kernel(q_ref, kc_ref, vc_ref, kp_ref, kb_ref, kn_ref, vp_ref, vb_ref, vn_ref, bias_ref, sink_ref,
                   o_ref, *, hd):
    q = jnp.concatenate([q_ref[0, g] for g in range(q_ref.shape[1])], axis=1)
    kcat = jnp.concatenate([kc_ref[0], kp_ref[0], kb_ref[0], kn_ref[0]], axis=0)
    vcat = jnp.concatenate([vc_ref[0], vp_ref[0], vb_ref[0], vn_ref[0]], axis=1)
    s = jnp.dot(kcat, q, preferred_element_type=jnp.float32) + bias_ref[0]
    sink = sink_ref[0]
    m = jnp.maximum(sink, jnp.max(s, axis=0, keepdims=True))
    p = jnp.exp2(s - m).astype(MXU_DTYPE)
    acc = jnp.dot(vcat, p, preferred_element_type=jnp.float32)
    l = acc[hd:hd + 1] + jnp.exp2(sink - m)
    o = (acc[:hd] / l).astype(o_ref.dtype)
    for g in range(o_ref.shape[1]):
        o_ref[0, g] = o[:, g * Q_BLOCK:(g + 1) * Q_BLOCK]


def _window_bias(tc, g):
    key_off = jnp.arange(-WINDOW, 2 * Q_BLOCK)[:, None]
    q_off = jnp.arange(Q_BLOCK)[None, :]
    band = jnp.abs(q_off - key_off) <= WINDOW
    variants = [band & (key_off >= 0), band, band & (key_off < Q_BLOCK)]
    lat = jnp.stack([jnp.where(v, 0.0, NEG_INF).astype(jnp.float32) for v in variants])
    lat = jnp.tile(lat, (1, 1, g))
    return jnp.concatenate([jnp.zeros((3, tc, g * Q_BLOCK), jnp.float32), lat], axis=1)


def _window(qt, kc, vc, kl, vl, sink_m, first_tile, tc, seq):
    kvh, g, hd, _ = qt.shape
    hv = vc.shape[1]
    m_lanes = g * Q_BLOCK
    nb = seq // Q_BLOCK
    assert nb >= 2
    nkeys = tc + 3 * Q_BLOCK
    kspec = lambda f: pl.BlockSpec((1, Q_BLOCK, hd), lambda h, b: (h, f(b), 0))
    vspec = lambda f: pl.BlockSpec((1, hv, Q_BLOCK), lambda h, b: (h, 0, f(b)))
    prev = lambda b: jnp.maximum(b - 1, 0)
    nxt = lambda b: jnp.minimum(b + 1, nb - 1)
    cur = lambda b: b
    variant = lambda b: jnp.where(b == 0, 0, jnp.where(b == nb - 1, 2, 1))
    return pl.pallas_call(
        functools.partial(_window_kernel, hd=hd),
        grid=(kvh, nb),
        in_specs=[pl.BlockSpec((1, g, hd, Q_BLOCK), lambda h, b: (h, 0, 0, b + first_tile)),
                  pl.BlockSpec((1, tc, hd), lambda h, b: (h, 0, 0)),
                  pl.BlockSpec((1, hv, tc), lambda h, b: (h, 0, 0)),
                  kspec(prev), kspec(cur), kspec(nxt), vspec(prev), vspec(cur), vspec(nxt),
                  pl.BlockSpec((1, nkeys, m_lanes), lambda h, b: (variant(b), 0, 0)),
                  pl.BlockSpec((1, 1, m_lanes), lambda h, b: (h, 0, 0))],
        out_specs=pl.BlockSpec((1, g, hd, Q_BLOCK), lambda h, b: (h, 0, 0, b)),
        out_shape=jax.ShapeDtypeStruct((kvh, g, hd, seq), MXU_DTYPE),
        compiler_params=_params("parallel", "parallel"),
        name="window_attn",
    )(qt, kc, vc, kl, kl, kl, vl, vl, vl, _window_bias(tc, g), sink_m)


def _rope_angles(seq, hd):
    d_axis = hd // 2
    inv_freq = ROPE_THETA ** (-jnp.arange(0, d_axis, 2, dtype=jnp.float32) / d_axis)
    t = jnp.arange(seq)
    row = (t // GRID_W).astype(jnp.float32)[:, None] * inv_freq
    col = (t % GRID_W).astype(jnp.float32)[:, None] * inv_freq
    return jnp.concatenate([row, col], axis=1)


def _rope_tables_t(seq, tc, pad, hd):
    ang = _rope_angles(seq, hd).T
    cos = jnp.concatenate([jnp.ones((hd // 2, tc), jnp.float32), jnp.cos(ang),
                           jnp.ones((hd // 2, pad), jnp.float32)], axis=1)
    sin = jnp.concatenate([jnp.zeros((hd // 2, tc), jnp.float32), jnp.sin(ang),
                           jnp.zeros((hd // 2, pad), jnp.float32)], axis=1)
    return cos, sin


def _rope_tables(seq, tc, pad, hd):
    ang = _rope_angles(seq, hd)
    cos = jnp.concatenate([jnp.cos(ang), jnp.cos(ang)], axis=1)
    sin = jnp.concatenate([-jnp.sin(ang), jnp.sin(ang)], axis=1)
    reps = V7X_LANES // hd
    cos = jnp.tile(cos, (1, reps))
    sin = jnp.tile(sin, (1, reps))
    cos = jnp.concatenate([jnp.ones((tc, V7X_LANES), jnp.float32), cos,
                           jnp.ones((pad, V7X_LANES), jnp.float32)], axis=0)
    sin = jnp.concatenate([jnp.zeros((tc, V7X_LANES), jnp.float32), sin,
                           jnp.zeros((pad, V7X_LANES), jnp.float32)], axis=0)
    return cos, sin


def _head_perm(hd):
    f = hd // 4
    idx = np.arange(hd).reshape(2, 2, f)
    return idx.transpose(1, 0, 2).reshape(hd)


def _v_aug(vt):
    kvh, _, n = vt.shape
    ones = jnp.ones((kvh, 1, n), vt.dtype)
    zeros = jnp.zeros((kvh, V7X_BF16_ROWS - 1, n), vt.dtype)
    return jnp.concatenate([vt, ones, zeros], axis=1)


def _chunk_keys(k, vt, tk):
    kvh, n, hd = k.shape
    kc = k.reshape(kvh, n // tk, tk, hd)
    vc = vt.reshape(kvh, vt.shape[1], n // tk, tk).transpose(0, 2, 1, 3)
    return kc, vc


def _attention_layer(xs, h, ht, gates, w_qkv, w_o, q_norm, k_norm, sink, hd, g, tc, seq, windowed, with_ctx_out):
    t_pad, d = h.shape
    t = tc + seq
    nheads = d // hd
    kvh = nheads // g
    qd, kd = nheads * hd, kvh * hd
    perm = _head_perm(hd)
    wqt = w_qkv[:, :qd].T.reshape(nheads, hd, d)[:, perm].reshape(qd, d).astype(MXU_DTYPE)
    wk = w_qkv[:, qd:qd + kd].reshape(d, kvh, hd)[:, :, perm].reshape(d, kd).astype(MXU_DTYPE)
    wvt = w_qkv[:, qd + kd:].T.astype(MXU_DTYPE)
    qscale = (hd ** -0.5) * LOG2E
    nwq = jnp.tile(q_norm[perm] * qscale, nheads)
    nwk = jnp.tile(k_norm[perm], kvh).reshape(1, kd)
    cos, sin = _rope_tables(seq, tc, t_pad - t, hd)
    cos_t, sin_t = _rope_tables_t(seq, tc, t_pad - t, hd)
    qt = _mm_qk_t(wqt, ht, nwq, cos_t, sin_t, hd, name="proj_q").reshape(kvh, g, hd, t_pad)
    k = _mm_qk(h, wk, nwk, cos, sin, hd, name="proj_k")[:t]
    vt = _mm(wvt, ht, MXU_DTYPE, tm_prefs=(512, 256, 128), tn_prefs=TOKEN_TILES, name="proj_v")

    m_lanes = 1024
    tq = m_lanes // g
    assert tc % tq == 0 and seq % tq == 0 and (not windowed or tq == Q_BLOCK)
    kk = k.reshape(t, kvh, hd).transpose(1, 0, 2)
    vt = _v_aug(vt.reshape(kvh, hd, t_pad)[:, :, :t])
    if sink is None:
        m_init = jnp.full((kvh, 1, m_lanes), NEG_INF, jnp.float32)
        init_l = 0.0
    else:
        m_init = jnp.repeat(sink.reshape(kvh, g).astype(jnp.float32) * LOG2E, tq, axis=1).reshape(kvh, 1, m_lanes)
        init_l = 1.0
    if windowed:
        o_lat = _window(qt, kk[:, :tc], vt[:, :, :tc], kk[:, tc:], vt[:, :, tc:], m_init, tc // tq, tc, seq)
    else:
        tk = _pick(t, (1280, 1024, 512, 256, 128))
        kc, vc = _chunk_keys(kk, vt, tk)
        o_lat = _flash(qt, kc, vc, m_init, init_l, tc // tq, seq // tq, name="global_attn")
    if with_ctx_out:
        kc, vc = _chunk_keys(kk[:, :tc], vt[:, :, :tc], tc)
        o_ctx = _flash(qt, kc, vc, m_init, init_l, 0, tc // tq, name="ctx_attn")
    else:
        o_ctx = jnp.zeros((kvh, g, hd, tc), MXU_DTYPE)
    o_pad = jnp.zeros((kvh, g, hd, t_pad - t), MXU_DTYPE)
    ot = jnp.concatenate([o_ctx, o_lat, o_pad], axis=3).reshape(qd, t_pad)
    return _mm_res_t(w_o.T.astype(MXU_DTYPE), ot, xs, gates, tc, name="mixer_out_t")


def _dft_parts(n, scale):
    k = np.arange(n)
    ang = 2.0 * np.pi * ((k[:, None] * k[None, :]) % n) / n
    return np.cos(ang) * scale, -np.sin(ang) * scale


def _chan_dft_kernel(x_ref, c_ref, s_ref, o_ref):
    x = x_ref[...]
    o_ref[0] = jnp.dot(x, c_ref[...], preferred_element_type=jnp.float32).astype(o_ref.dtype)
    o_ref[1] = jnp.dot(x, s_ref[...], preferred_element_type=jnp.float32).astype(o_ref.dtype)


def _chan_dft(h, gc):
    t, d = h.shape
    re, im = _dft_parts(gc, gc ** -0.5)
    tm = _pick(t, ROW_TILES)
    return pl.pallas_call(
        _chan_dft_kernel,
        grid=(t // tm, d // gc),
        in_specs=[pl.BlockSpec((tm, gc), lambda i, j: (i, j)),
                  pl.BlockSpec((gc, gc), lambda i, j: (0, 0)),
                  pl.BlockSpec((gc, gc), lambda i, j: (0, 0))],
        out_specs=pl.BlockSpec((2, tm, gc), lambda i, j: (0, i, j)),
        out_shape=jax.ShapeDtypeStruct((2, t, d), MXU_DTYPE),
        compiler_params=_params("parallel", "parallel"),
        name="chan_dft",
    )(h, jnp.asarray(re, MXU_DTYPE), jnp.asarray(im, MXU_DTYPE))


def _pos_stage1_kernel(m_ref, y_ref, tr_ref, ti_ref, o_ref, *, n1):
    y = jnp.concatenate([y_ref[0], y_ref[1]], axis=0)
    a = jnp.dot(m_ref[...], y, preferred_element_type=jnp.float32)
    ar, ai = a[:n1], a[n1:]
    tr, ti = tr_ref[0][:, 0:1], ti_ref[0][:, 0:1]
    o_ref[0] = (ar * tr - ai * ti).astype(o_ref.dtype)
    o_ref[1] = (ar * ti + ai * tr).astype(o_ref.dtype)


def _pos_stage2_kernel(m_ref, b_ref, o_ref):
    b = jnp.concatenate([b_ref[0, 0], b_ref[1, 0]], axis=0)
    o_ref[0] = jnp.dot(m_ref[...], b, preferred_element_type=jnp.float32).astype(o_ref.dtype)


def _pos_dft_two_stage(y, n1, n2):
    _, s, d = y.shape
    sc = float(s) ** -0.25
    r1, i1 = _dft_parts(n1, sc)
    m1 = np.block([[r1, -i1], [i1, r1]])
    f1 = np.arange(n1)[None, :]
    t2 = np.arange(n2)[:, None]
    ang = 2.0 * np.pi * (t2 * f1) / s
    tw_r = np.broadcast_to(np.cos(ang)[:, :, None], (n2, n1, V7X_LANES))
    tw_i = np.broadcast_to(-np.sin(ang)[:, :, None], (n2, n1, V7X_LANES))
    y4 = y.reshape(2, n1, n2 * d)
    b = pl.pallas_call(
        functools.partial(_pos_stage1_kernel, n1=n1),
        grid=(n2,),
        in_specs=[pl.BlockSpec((2 * n1, 2 * n1), lambda j: (0, 0)),
                  pl.BlockSpec((2, n1, d), lambda j: (0, 0, j)),
                  pl.BlockSpec((1, n1, V7X_LANES), lambda j: (j, 0, 0)),
                  pl.BlockSpec((1, n1, V7X_LANES), lambda j: (j, 0, 0))],
        out_specs=pl.BlockSpec((2, n1, d), lambda j: (0, 0, j)),
        out_shape=jax.ShapeDtypeStruct((2, n1, n2 * d), MXU_DTYPE),
        compiler_params=_params("parallel"),
        name="pos_dft_stage1",
    )(jnp.asarray(m1, MXU_DTYPE), y4, jnp.asarray(tw_r, jnp.float32), jnp.asarray(tw_i, jnp.float32))
    b4 = b.reshape(2, n1, n2, d)
    r2, i2 = _dft_parts(n2, sc)
    m2 = np.concatenate([r2, -i2], axis=1)
    tcn = _pick(d, (2048, 1024, 512, 256, 128))
    out = pl.pallas_call(
        _pos_stage2_kernel,
        grid=(n1, d // tcn),
        in_specs=[pl.BlockSpec((n2, 2 * n2), lambda f, j: (0, 0)),
                  pl.BlockSpec((2, 1, n2, tcn), lambda f, j: (0, f, 0, j))],
        out_specs=pl.BlockSpec((1, n2, tcn), lambda f, j: (f, 0, j)),
        out_shape=jax.ShapeDtypeStruct((n1, n2, d), MXU_DTYPE),
        compiler_params=_params("parallel", "parallel"),
        name="pos_dft_stage2",
    )(jnp.asarray(m2, MXU_DTYPE), b4)
    return out.transpose(1, 0, 2).reshape(s, d)


def _fourier_mix(h, tc, seq):
    t_pad, d = h.shape
    y = _chan_dft(h, d // FNET_GROUPS)
    rc, ic = _dft_parts(tc, tc ** -0.5)
    mc = np.concatenate([rc, -ic], axis=1)
    yc = jnp.concatenate([y[0, :tc], y[1, :tc]], axis=0)
    mixed_c = _mm(jnp.asarray(mc, MXU_DTYPE), yc, MXU_DTYPE, name="ctx_pos_dft")
    n1 = 1 << (int(math.log2(seq)) // 2)
    n2 = seq // n1
    mixed_l = _pos_dft_two_stage(y[:, tc:tc + seq], n1, n2)
    return jnp.concatenate([mixed_c, mixed_l, jnp.zeros((t_pad - tc - seq, d), MXU_DTYPE)], axis=0)


def _top16(s):
    n = s.shape[0]
    rows = lax.broadcasted_iota(jnp.int32, s.shape, 0)
    row16 = lax.broadcasted_iota(jnp.int32, (PEER_TOPK, s.shape[1]), 0)
    rank = jnp.full(s.shape, RANK_NONE, jnp.float32)
    sv = jnp.zeros((PEER_TOPK, s.shape[1]), jnp.float32)
    v = s
    for k in range(PEER_TOPK):
        m = jnp.max(v, axis=0, keepdims=True)
        idx = jnp.min(jnp.where(v == m, rows, n), axis=0, keepdims=True)
        hit = rows == idx
        rank = jnp.where(hit, float(k), rank)
        v = jnp.where(hit, -jnp.inf, v)
        sv = jnp.where(row16 == k, m, sv)
    return rank, sv


_CAND_GROUPS = ((0, 0, 8), (0, 8, 8), (1, 0, 8), (2, 0, 5), (3, 0, 4), (4, 0, 3), (5, 0, 2), (6, 0, 2), (7, 0, 2))


def _select_pairs(sv0, sv1):
    lanes = sv0.shape[1]
    sub = lax.broadcasted_iota(jnp.int32, (V7X_SUBLANES, lanes), 0)
    cands, poss = [], []
    for r0, r1_0, cnt in _CAND_GROUPS:
        c = sv0[r0:r0 + 1] + sv1[r1_0:r1_0 + V7X_SUBLANES]
        cands.append(jnp.where(sub < cnt, c, -jnp.inf))
        poss.append(r0 * PEER_TOPK + r1_0 + sub)
    cands.append(sv0[V7X_SUBLANES:] + sv1[0:1])
    poss.append((V7X_SUBLANES + sub) * PEER_TOPK)
    cand = jnp.concatenate(cands, axis=0)
    pos = jnp.concatenate(poss, axis=0)
    cmax = sv0[0:1] + sv1[0:1]
    big = PEER_TOPK * PEER_TOPK
    c = cand
    sel = jnp.zeros(cand.shape, jnp.float32)
    for _ in range(PEER_TOPK):
        m = jnp.max(c, axis=0, keepdims=True)
        pidx = jnp.min(jnp.where(c == m, pos, big), axis=0, keepdims=True)
        hit = pos == pidx
        sel = jnp.where(hit, 1.0, sel)
        c = jnp.where(hit, -jnp.inf, c)
    z = jnp.sum(jnp.where(sel > 0.0, jnp.exp(cand - cmax), 0.0), axis=0, keepdims=True)
    g = V7X_SUBLANES
    rows = [jnp.sum(sel[0:2 * g], axis=0, keepdims=True)]
    for i in range(2, len(_CAND_GROUPS)):
        rows.append(jnp.sum(sel[i * g:(i + 1) * g], axis=0, keepdims=True))
    rows.append(sel[len(_CAND_GROUPS) * g:])
    return jnp.concatenate(rows, axis=0), z


def _peer_select_kernel(q_ref, keys_ref, r1_ref, b_ref, cnt_ref, a_ref):
    def head(hh, carry):
        base = pl.multiple_of(hh * 2 * PEER_HALF, 2 * PEER_HALF)
        q0 = q_ref[pl.ds(base, PEER_HALF), :]
        q1 = q_ref[pl.ds(base + PEER_HALF, PEER_HALF), :]
        s0 = jnp.dot(keys_ref[hh, 0], q0, preferred_element_type=jnp.float32, precision=lax.Precision.HIGHEST)
        s1 = jnp.dot(keys_ref[hh, 1], q1, preferred_element_type=jnp.float32, precision=lax.Precision.HIGHEST)
        rank0, sv0 = _top16(s0)
        rank1, sv1 = _top16(s1)
        cnt16, z = _select_pairs(sv0, sv1)
        cnt = jnp.zeros(s0.shape, jnp.float32)
        for r in range(PEER_TOPK):
            cnt = jnp.where(rank0 == float(r), cnt16[r:r + 1], cnt)
        r1_ref[hh] = rank1
        b_ref[hh] = jnp.exp(s1 - sv1[0:1])
        cnt_ref[hh] = cnt
        a_ref[hh] = jnp.exp(s0 - sv0[0:1]) / z
        return carry

    lax.fori_loop(0, PEER_HEADS, head, 0)


def _peer_select(qt, sub_keys):
    _, t = qt.shape
    tl = _pick(t, (2 * V7X_LANES, V7X_LANES))
    shp = jax.ShapeDtypeStruct((PEER_HEADS, PEER_N_KEYS, t), jnp.float32)
    ospec = pl.BlockSpec((PEER_HEADS, PEER_N_KEYS, tl), lambda i: (0, 0, i))
    return pl.pallas_call(
        _peer_select_kernel,
        grid=(t // tl,),
        in_specs=[pl.BlockSpec((qt.shape[0], tl), lambda i: (0, i)),
                  pl.BlockSpec(sub_keys.shape, lambda i: (0, 0, 0, 0))],
        out_specs=[ospec] * 4,
        out_shape=[shp] * 4,
        compiler_params=_params("parallel"),
        name="peer_select",
    )(qt, sub_keys)


def _gelu_exact(x):
    return 0.5 * x * (1.0 + lax.erf(x * (2.0 ** -0.5)))


def _peer_act_kernel(u_ref, h_ref, r1_ref, b_ref, cnt_ref, a_ref, o_ref, *, iblocks):
    nk = PEER_N_KEYS
    groups = [slice(g * MXU_COLS, (g + 1) * MXU_COLS) for g in range(h_ref.shape[1] // MXU_COLS)]
    pres = [jnp.dot(u_ref[...], h_ref[:, cols], preferred_element_type=jnp.float32) for cols in groups]
    dt = o_ref.dtype
    for pre, cols in zip(pres, groups):
        r1 = [r1_ref[hh, :, cols].astype(dt) for hh in range(PEER_HEADS)]
        b = [b_ref[hh, :, cols].astype(dt) for hh in range(PEER_HEADS)]
        for ii in range(iblocks):
            w = jnp.zeros((nk, MXU_COLS), dt)
            for hh in range(PEER_HEADS):
                sel = r1[hh] < cnt_ref[hh, ii:ii + 1, cols].astype(dt)
                w = w + jnp.where(sel, b[hh], jnp.zeros_like(b[hh])) * a_ref[hh, ii:ii + 1, cols].astype(dt)
            o_ref[ii * nk:(ii + 1) * nk, cols] = _gelu_exact(pre[ii * nk:(ii + 1) * nk]).astype(dt) * w


def _peer_act(u, ht, r1, b, cnt, a):
    e, d = u.shape
    _, t = ht.shape
    iblocks = V7X_SUBLANES
    te = iblocks * PEER_N_KEYS
    tt = _pick(t, TOKEN_TILES)
    full = pl.BlockSpec((PEER_HEADS, PEER_N_KEYS, tt), lambda i, j: (0, 0, i))
    part = pl.BlockSpec((PEER_HEADS, iblocks, tt), lambda i, j: (0, j, i))
    return pl.pallas_call(
        functools.partial(_peer_act_kernel, iblocks=iblocks),
        grid=(t // tt, e // te),
        in_specs=[pl.BlockSpec((te, d), lambda i, j: (j, 0)),
                  pl.BlockSpec((d, tt), lambda i, j: (0, i)),
                  full, full, part, part],
        out_specs=pl.BlockSpec((te, tt), lambda i, j: (j, i)),
        out_shape=jax.ShapeDtypeStruct((e, t), MXU_DTYPE),
        compiler_params=_params("parallel", "parallel"),
        name="peer_act",
    )(u, ht, r1, b, cnt, a)


def _peer(x, nw, mod_ctx, mod_lat, gates, tc, w_q, sub_keys, u_tab, v_tab):
    (ht,) = _modulate(x, nw, mod_ctx, mod_lat, tc, ("feature_major",))
    qt = _mm(w_q.T.astype(MXU_DTYPE), ht, jnp.float32, tm_prefs=(512, 256, 128),
             tn_prefs=TOKEN_TILES, name="peer_query")
    r1, b, cnt, a = _peer_select(qt, sub_keys)
    act = _peer_act(u_tab.astype(MXU_DTYPE), ht, r1, b, cnt, a)
    return _mm_res_t(v_tab.T.astype(MXU_DTYPE), act, x, gates, tc, name="peer_out")


def kernel(x, c, ctx, c_ctx, ada_w, ada_b, mix_norm, ffn_norm, a_w_qkv, a_w_o, a_q_norm, a_k_norm,
           b_w_qkv, b_w_o, b_q_norm, b_k_norm, b_sink, f_w_out, peer_w_q, peer_sub_keys, peer_u, peer_v):
    batch, seq, d = x.shape
    assert batch == 1 and ctx.shape[0] == 1
    tc = ctx.shape[1]
    depth = ada_w.shape[0]
    pad = -(tc + seq) % TOKEN_TILES[0]
    xs = jnp.concatenate([ctx[0], x[0], jnp.zeros((pad, d), x.dtype)], axis=0)
    mods = _ada(jnp.stack([c[0], c_ctx]), ada_w, ada_b)
    mods = mods.reshape(depth, 2, N_MOD, d)
    for layer in range(depth):
        last = layer == depth - 1
        kind, j = layer % N_MIXERS, layer // N_MIXERS
        lat, cx = mods[layer, 0], mods[layer, 1]
        gates = jnp.stack([cx[2], lat[2]])
        if kind == 2:
            (h,) = _modulate(xs, mix_norm[layer], cx[0:2], lat[0:2], tc, ("token_major",))
            xs = _mm_res(_fourier_mix(h, tc, seq), f_w_out[j].astype(MXU_DTYPE), xs, gates, tc, name="mixer_out")
        else:
            h, ht = _modulate(xs, mix_norm[layer], cx[0:2], lat[0:2], tc, ("token_major", "feature_major"))
            if kind == 0:
                xs = _attention_layer(xs, h, ht, gates, a_w_qkv[j], a_w_o[j], a_q_norm[j], a_k_norm[j], None,
                                      A_HEAD_DIM, A_GROUP, tc, seq, False, not last)
            else:
                xs = _attention_layer(xs, h, ht, gates, b_w_qkv[j], b_w_o[j], b_q_norm[j], b_k_norm[j], b_sink[j],
                                      B_HEAD_DIM, B_GROUP, tc, seq, True, not last)
        xs = _peer(xs, ffn_norm[layer], cx[3:5], lat[3:5], jnp.stack([cx[5], lat[5]]), tc,
                   peer_w_q[layer], peer_sub_keys[layer], peer_u[layer], peer_v[layer])
    return xs[tc:tc + seq][None]
```

```python
import functools
import math

import numpy as np
import jax
import jax.numpy as jnp
from jax import lax
from jax.experimental import pallas as pl
from jax.experimental.pallas import tpu as pltpu

GRID_W = 64
Q_BLOCK = 128
ROPE_THETA = 10000.0
NORM_EPS = 1e-6
NEG_INF = -1e30
N_MOD = 6
N_MIXERS = 3
A_HEAD_DIM = 128
A_GROUP = 4
B_HEAD_DIM = 64
B_GROUP = 8
WINDOW = 128
FNET_GROUPS = 8
PEER_HEADS = 8
PEER_N_KEYS = 96
PEER_TOPK = 16
PEER_HALF = 128

V7X_LANES = 128
V7X_SUBLANES = 8
V7X_BF16_ROWS = 16
MXU_COLS = 256
V7X_VMEM_BYTES = 64 * 1024 * 1024
VMEM_LIMIT = V7X_VMEM_BYTES - 8 * 1024 * 1024

MXU_DTYPE = jnp.bfloat16
LOG2E = 1.4426950408889634
RANK_NONE = 99.0
ROW_TILES = (1536, 768, 512, 256, 128)
TOKEN_TILES = (768, 512, 256)
FLASH_KEY_CHUNKS = (2048, 1024, 512, 256)
EXP2_HEADROOM = 64.0


def _params(*sem):
    return pltpu.CompilerParams(dimension_semantics=sem, vmem_limit_bytes=VMEM_LIMIT)


def _pick(n, prefs):
    for p in prefs:
        if n % p == 0:
            return p
    return n


def _ada_kernel(cond_ref, w_ref, b_ref, o_ref):
    w = w_ref[0]
    reps = w.shape[1] // V7X_LANES
    rows = []
    for r in range(2):
        c = cond_ref[r]
        s = c * jax.nn.sigmoid(c)
        rows.append(jnp.sum(w * jnp.tile(s, (1, reps)), axis=0, keepdims=True))
    o_ref[0] = jnp.concatenate(rows, axis=0) + b_ref[0]


def _ada(cond2, ada_w, ada_b):
    depth, d, n = ada_w.shape
    tn = _pick(n, (512, 256, 128))
    cond_rep = jnp.broadcast_to(cond2[:, :, None], (2, d, V7X_LANES))
    return pl.pallas_call(
        _ada_kernel,
        grid=(depth, n // tn),
        in_specs=[pl.BlockSpec((2, d, V7X_LANES), lambda l, j: (0, 0, 0)),
                  pl.BlockSpec((1, d, tn), lambda l, j: (l, 0, j)),
                  pl.BlockSpec((1, 1, tn), lambda l, j: (l, 0, j))],
        out_specs=pl.BlockSpec((1, 2, tn), lambda l, j: (l, 0, j)),
        out_shape=jax.ShapeDtypeStruct((depth, 2, n), jnp.float32),
        compiler_params=_params("parallel", "parallel"),
        name="ada",
    )(cond_rep, ada_w, ada_b.reshape(depth, 1, n))


def _modulate_kernel(x_ref, nw_ref, mod_ref, *o_refs, layouts):
    x = x_ref[...]
    y = x * lax.rsqrt(jnp.mean(x * x, axis=-1, keepdims=True) + NORM_EPS) * nw_ref[...]
    shift = mod_ref[0, 0:1, :]
    scale = mod_ref[0, 1:2, :]
    h = y * (1.0 + scale) + shift
    for o_ref, layout in zip(o_refs, layouts):
        o_ref[...] = (h.T if layout == "feature_major" else h).astype(o_ref.dtype)


def _modulate(x, nw, mod_ctx, mod_lat, tc, layouts):
    t, d = x.shape
    tm = _pick(math.gcd(t, tc), (256, 128))
    ctx_tiles = tc // tm
    mods = jnp.stack([mod_ctx, mod_lat])
    shapes = [jax.ShapeDtypeStruct((d, t) if lay == "feature_major" else (t, d), MXU_DTYPE) for lay in layouts]
    specs = [pl.BlockSpec((d, tm), lambda i: (0, i)) if lay == "feature_major"
             else pl.BlockSpec((tm, d), lambda i: (i, 0)) for lay in layouts]
    return pl.pallas_call(
        functools.partial(_modulate_kernel, layouts=tuple(layouts)),
        grid=(t // tm,),
        in_specs=[pl.BlockSpec((tm, d), lambda i: (i, 0)),
                  pl.BlockSpec((1, d), lambda i: (0, 0)),
                  pl.BlockSpec((1, 2, d), lambda i: (jnp.where(i < ctx_tiles, 0, 1), 0, 0))],
        out_specs=specs,
        out_shape=shapes,
        compiler_params=_params("parallel"),
        name="modulate",
    )(x, nw.reshape(1, d), mods)


def _mm_kernel(a_ref, b_ref, o_ref):
    o_ref[...] = jnp.dot(a_ref[...], b_ref[...], preferred_element_type=jnp.float32).astype(o_ref.dtype)


def _mm(a, b, out_dtype, *, tm_prefs=ROW_TILES, tn_prefs=(512, 256, 128), name="mm"):
    m, k = a.shape
    _, n = b.shape
    tm, tn = _pick(m, tm_prefs), _pick(n, tn_prefs)
    return pl.pallas_call(
        _mm_kernel,
        grid=(m // tm, n // tn),
        in_specs=[pl.BlockSpec((tm, k), lambda i, j: (i, 0)),
                  pl.BlockSpec((k, tn), lambda i, j: (0, j))],
        out_specs=pl.BlockSpec((tm, tn), lambda i, j: (i, j)),
        out_shape=jax.ShapeDtypeStruct((m, n), out_dtype),
        compiler_params=_params("parallel", "parallel"),
        name=name,
    )(a, b)


def _mm_res_kernel(a_ref, b_ref, res_ref, g_ref, o_ref, *, tc, transposed):
    acc = jnp.dot(a_ref[...], b_ref[...], preferred_element_type=jnp.float32)
    if transposed:
        acc = acc.T
        row0 = pl.program_id(0) * acc.shape[0]
    else:
        row0 = pl.program_id(0) * acc.shape[0]
    rid = row0 + lax.broadcasted_iota(jnp.int32, acc.shape, 0)
    gate = jnp.where(rid < tc, g_ref[0:1, :], g_ref[1:2, :])
    o_ref[...] = res_ref[...] + gate * acc


def _mm_res(a, b, res, gates, tc, *, name="mm_res"):
    m, k = a.shape
    _, n = b.shape
    tm, tn = _pick(m, ROW_TILES), _pick(n, (512, 256, 128))
    return pl.pallas_call(
        functools.partial(_mm_res_kernel, tc=tc, transposed=False),
        grid=(m // tm, n // tn),
        in_specs=[pl.BlockSpec((tm, k), lambda i, j: (i, 0)),
                  pl.BlockSpec((k, tn), lambda i, j: (0, j)),
                  pl.BlockSpec((tm, tn), lambda i, j: (i, j)),
                  pl.BlockSpec((2, tn), lambda i, j: (0, j))],
        out_specs=pl.BlockSpec((tm, tn), lambda i, j: (i, j)),
        out_shape=jax.ShapeDtypeStruct((m, n), jnp.float32),
        compiler_params=_params("parallel", "parallel"),
        name=name,
    )(a, b, res, gates)


def _mm_res_t(vt, at, res, gates, tc, *, layer=None, name="mm_res_t"):
    n, k = vt.shape[-2:]
    _, t = at.shape
    vspec = (lambda tn: pl.BlockSpec((tn, k), lambda i, j: (j, 0))) if layer is None else (
        lambda tn: pl.BlockSpec((None, tn, k), lambda i, j: (layer, j, 0)))
    tt, tn = _pick(t, TOKEN_TILES), _pick(n, (512, 256, 128) if k <= 4096 else (256, 128))
    return pl.pallas_call(
        functools.partial(_mm_res_kernel, tc=tc, transposed=True),
        grid=(t // tt, n // tn),
        in_specs=[vspec(tn),
                  pl.BlockSpec((k, tt), lambda i, j: (0, i)),
                  pl.BlockSpec((tt, tn), lambda i, j: (i, j)),
                  pl.BlockSpec((2, tn), lambda i, j: (0, j))],
        out_specs=pl.BlockSpec((tt, tn), lambda i, j: (i, j)),
        out_shape=jax.ShapeDtypeStruct((t, n), jnp.float32),
        compiler_params=_params("parallel", "parallel"),
        name=name,
    )(vt, at, res, gates)


def _mm_qk_kernel(a_ref, b_ref, nw_ref, cos_ref, sin_ref, o_ref, *, hd):
    gw = min(MXU_COLS, b_ref.shape[1])
    accs = [jnp.dot(a_ref[...], b_ref[:, g * gw:(g + 1) * gw], preferred_element_type=jnp.float32)
            for g in range(b_ref.shape[1] // gw)]
    cos = cos_ref[...]
    sin = sin_ref[...]
    lane = lax.broadcasted_iota(jnp.int32, (a_ref.shape[0], V7X_LANES), 1)
    per_group = gw // V7X_LANES
    for c in range(b_ref.shape[1] // V7X_LANES):
        sub = c % per_group
        blk = accs[c // per_group][:, sub * V7X_LANES:(sub + 1) * V7X_LANES]
        sq = blk * blk
        if hd == V7X_LANES:
            ms = jnp.sum(sq, axis=-1, keepdims=True) * (1.0 / hd)
        else:
            lo = jnp.sum(jnp.where(lane < hd, sq, 0.0), axis=-1, keepdims=True)
            hi = jnp.sum(jnp.where(lane < hd, 0.0, sq), axis=-1, keepdims=True)
            ms = jnp.where(lane < hd, lo, hi) * (1.0 / hd)
        y = blk * lax.rsqrt(ms + NORM_EPS) * nw_ref[:, c * V7X_LANES:(c + 1) * V7X_LANES]
        if hd == V7X_LANES:
            partner = pltpu.roll(y, hd // 2, 1)
        else:
            partner = jnp.where(lane % hd < hd // 2,
                                pltpu.roll(y, V7X_LANES - hd // 2, 1), pltpu.roll(y, hd // 2, 1))
        o_ref[:, c * V7X_LANES:(c + 1) * V7X_LANES] = (y * cos + partner * sin).astype(o_ref.dtype)


def _mm_qk(a, b, nw, cos, sin, hd, *, name):
    m, k = a.shape
    _, n = b.shape
    tm, tn = _pick(m, ROW_TILES), _pick(n, (512, 256, 128))
    return pl.pallas_call(
        functools.partial(_mm_qk_kernel, hd=hd),
        grid=(m // tm, n // tn),
        in_specs=[pl.BlockSpec((tm, k), lambda i, j: (i, 0)),
                  pl.BlockSpec((k, tn), lambda i, j: (0, j)),
                  pl.BlockSpec((1, tn), lambda i, j: (0, j)),
                  pl.BlockSpec((tm, V7X_LANES), lambda i, j: (i, 0)),
                  pl.BlockSpec((tm, V7X_LANES), lambda i, j: (i, 0))],
        out_specs=pl.BlockSpec((tm, tn), lambda i, j: (i, j)),
        out_shape=jax.ShapeDtypeStruct((m, n), MXU_DTYPE),
        compiler_params=_params("parallel", "parallel"),
        name=name,
    )(a, b, nw, cos, sin)


def _mm_qk_t_kernel(w_ref, h_ref, nw_ref, cos_ref, sin_ref, o_ref, *, hd):
    acc = jnp.dot(w_ref[...], h_ref[...], preferred_element_type=jnp.float32)
    cos = cos_ref[...]
    sin = sin_ref[...]
    reps = acc.shape[1] // V7X_LANES
    half = hd // 2
    for r in range(acc.shape[0] // hd):
        x = acc[r * hd:(r + 1) * hd]
        ms = jnp.sum(x * x, axis=0, keepdims=True) * (1.0 / hd)
        y = x * lax.rsqrt(ms + NORM_EPS) * jnp.tile(nw_ref[r * hd:(r + 1) * hd], (1, reps))
        y1, y2 = y[:half], y[half:]
        o_ref[r * hd:r * hd + half] = (y1 * cos - y2 * sin).astype(o_ref.dtype)
        o_ref[r * hd + half:(r + 1) * hd] = (y1 * sin + y2 * cos).astype(o_ref.dtype)


def _mm_qk_t(wt, ht, nw, cos, sin, hd, *, name):
    n, k = wt.shape
    _, t = ht.shape
    tf, tt = _pick(n, (512, 256, 128)), _pick(t, TOKEN_TILES)
    nw_rep = jnp.broadcast_to(nw[:, None], (n, V7X_LANES))
    return pl.pallas_call(
        functools.partial(_mm_qk_t_kernel, hd=hd),
        grid=(t // tt, n // tf),
        in_specs=[pl.BlockSpec((tf, k), lambda i, j: (j, 0)),
                  pl.BlockSpec((k, tt), lambda i, j: (0, i)),
                  pl.BlockSpec((tf, V7X_LANES), lambda i, j: (j, 0)),
                  pl.BlockSpec((hd // 2, tt), lambda i, j: (0, i)),
                  pl.BlockSpec((hd // 2, tt), lambda i, j: (0, i))],
        out_specs=pl.BlockSpec((tf, tt), lambda i, j: (j, i)),
        out_shape=jax.ShapeDtypeStruct((n, t), MXU_DTYPE),
        compiler_params=_params("parallel", "parallel"),
        name=name,
    )(wt, ht, nw_rep, cos, sin)


def _flash_kernel(q_ref, kc_ref, vc_ref, *rest, init_l, n_lat):
    if n_lat:
        kl_ref, vl_ref, m0_ref, o_ref, mrun_ref, macc_ref, l_ref, acc_ref, knorm_ref = rest
    else:
        m0_ref, o_ref, mrun_ref, macc_ref, l_ref, acc_ref, knorm_ref = rest
    q = jnp.concatenate([q_ref[0, g] for g in range(q_ref.shape[1])], axis=1)

    def norm2(k):
        kf = k.astype(jnp.float32)
        return jnp.max(jnp.sum(kf * kf, axis=1, keepdims=True), axis=0, keepdims=True)

    if n_lat:
        @pl.when(pl.program_id(1) == 0)
        def _():
            k2 = lax.fori_loop(0, n_lat, lambda c, mx: jnp.maximum(mx, norm2(kl_ref[0, c])), norm2(kc_ref[0]))
            knorm_ref[...] = jnp.broadcast_to(jnp.sqrt(k2), knorm_ref.shape)

    mrun_ref[...] = m0_ref[0]
    macc_ref[...] = m0_ref[0]
    l_ref[...] = jnp.full(l_ref.shape, init_l, jnp.float32)
    acc_ref[...] = jnp.zeros(acc_ref.shape, jnp.float32)

    def accumulate(p, v, m_to):
        alpha = jnp.exp2(macc_ref[...] - m_to)
        l_ref[...] = l_ref[...] * alpha + jnp.sum(p, axis=0, keepdims=True)
        acc_ref[...] = acc_ref[...] * alpha + jnp.dot(v, p.astype(MXU_DTYPE), preferred_element_type=jnp.float32)
        macc_ref[...] = m_to

    def own_max(k, v):
        s = jnp.dot(k, q, preferred_element_type=jnp.float32)
        m_new = jnp.maximum(mrun_ref[...], jnp.max(s, axis=0, keepdims=True))
        accumulate(jnp.exp2(s - m_new), v, m_new)
        mrun_ref[...] = m_new

    def earlier_max(k, v):
        m_use = mrun_ref[...]
        s = jnp.dot(k, q, preferred_element_type=jnp.float32)
        accumulate(jnp.exp2(s - m_use), v, m_use)
        mrun_ref[...] = jnp.maximum(m_use, jnp.max(s, axis=0, keepdims=True))

    own_max(kc_ref[0], vc_ref[0])
    if n_lat:
        qf = q.astype(jnp.float32)
        qnorm = jnp.sqrt(jnp.sum(qf * qf, axis=0, keepdims=True))
        bounded = jnp.max(qnorm * knorm_ref[...] - mrun_ref[...]) <= EXP2_HEADROOM

        def loop(step):
            def body(c, carry):
                step(kl_ref[0, c], vl_ref[0, c])
                return carry
            lax.fori_loop(0, n_lat, body, 0)

        pl.when(bounded)(lambda: loop(earlier_max))
        pl.when(jnp.logical_not(bounded))(lambda: loop(own_max))

    o = (acc_ref[...] / l_ref[...]).astype(o_ref.dtype)
    tq = o_ref.shape[3]
    for g in range(o_ref.shape[1]):
        o_ref[0, g] = o[:, g * tq:(g + 1) * tq]


def _flash(qt, kc, vc, kl, vl, m0, init_l, first_tile, n_tiles, *, name):
    kvh, g, hd, _ = qt.shape
    tc = kc.shape[1]
    m_lanes = m0.shape[2]
    tq = m_lanes // g
    n_lat = 0 if kl is None else kl.shape[1]
    lat_specs, lat_args = [], []
    if n_lat:
        tk = kl.shape[2]
        lat_specs = [pl.BlockSpec((1, n_lat, tk, hd), lambda h, i: (h, 0, 0, 0)),
                     pl.BlockSpec((1, n_lat, hd, tk), lambda h, i: (h, 0, 0, 0))]
        lat_args = [kl, vl]
    return pl.pallas_call(
        functools.partial(_flash_kernel, init_l=init_l, n_lat=n_lat),
        grid=(kvh, n_tiles),
        in_specs=[pl.BlockSpec((1, g, hd, tq), lambda h, i: (h, 0, 0, i + first_tile)),
                  pl.BlockSpec((1, tc, hd), lambda h, i: (h, 0, 0)),
                  pl.BlockSpec((1, hd, tc), lambda h, i: (h, 0, 0)),
                  *lat_specs,
                  pl.BlockSpec((1, 1, m_lanes), lambda h, i: (h, 0, 0))],
        out_specs=pl.BlockSpec((1, g, hd, tq), lambda h, i: (h, 0, 0, i)),
        out_shape=jax.ShapeDtypeStruct((kvh, g, hd, n_tiles * tq), MXU_DTYPE),
        scratch_shapes=[pltpu.VMEM((1, m_lanes), jnp.float32),
                        pltpu.VMEM((1, m_lanes), jnp.float32),
                        pltpu.VMEM((1, m_lanes), jnp.float32),
                        pltpu.VMEM((hd, m_lanes), jnp.float32),
                        pltpu.VMEM((1, m_lanes), jnp.float32)],
        compiler_params=_params("parallel", "arbitrary"),
        name=name,
    )(qt, kc, vc, *lat_args, m0)


def _window_kernel(q_ref, kc_ref, vc_ref, kp_ref, kb_ref, kn_ref, vp_ref, vb_ref, vn_ref, bias_ref, sink_ref,
                   o_ref, *, hd):
    q = jnp.concatenate([q_ref[0, g] for g in range(q_ref.shape[1])], axis=1)
    kcat = jnp.concatenate([kc_ref[0], kp_ref[0], kb_ref[0], kn_ref[0]], axis=0)
    vcat = jnp.concatenate([vc_ref[0], vp_ref[0], vb_ref[0], vn_ref[0]], axis=1)
    s = jnp.dot(kcat, q, preferred_element_type=jnp.float32) + bias_ref[0]
    sink = sink_ref[0]
    m = jnp.maximum(sink, jnp.max(s, axis=0, keepdims=True))
    p = jnp.exp2(s - m).astype(MXU_DTYPE)
    acc = jnp.dot(vcat, p, preferred_element_type=jnp.float32)
    l = acc[hd:hd + 1] + jnp.exp2(sink - m)
    o = (acc[:hd] / l).astype(o_ref.dtype)
    for g in range(o_ref.shape[1]):
        o_ref[0, g] = o[:, g * Q_BLOCK:(g + 1) * Q_BLOCK]


def _window_bias(tc, g):
    key_off = jnp.arange(-WINDOW, 2 * Q_BLOCK)[:, None]
    q_off = jnp.arange(Q_BLOCK)[None, :]
    band = jnp.abs(q_off - key_off) <= WINDOW
    variants = [band & (key_off >= 0), band, band & (key_off < Q_BLOCK)]
    lat = jnp.stack([jnp.where(v, 0.0, NEG_INF).astype(jnp.float32) for v in variants])
    lat = jnp.tile(lat, (1, 1, g))
    return jnp.concatenate([jnp.zeros((3, tc, g * Q_BLOCK), jnp.float32), lat], axis=1)


def _window(qt, kc, vc, kl, vl, sink_m, first_tile, tc, seq):
    kvh, g, hd, _ = qt.shape
    hv = vc.shape[1]
    m_lanes = g * Q_BLOCK
    nb = seq // Q_BLOCK
    assert nb >= 2
    nkeys = tc + 3 * Q_BLOCK
    kspec = lambda f: pl.BlockSpec((1, Q_BLOCK, hd), lambda h, b: (h, f(b), 0))
    vspec = lambda f: pl.BlockSpec((1, hv, Q_BLOCK), lambda h, b: (h, 0, f(b)))
    prev = lambda b: jnp.maximum(b - 1, 0)
    nxt = lambda b: jnp.minimum(b + 1, nb - 1)
    cur = lambda b: b
    variant = lambda b: jnp.where(b == 0, 0, jnp.where(b == nb - 1, 2, 1))
    return pl.pallas_call(
        functools.partial(_window_kernel, hd=hd),
        grid=(kvh, nb),
        in_specs=[pl.BlockSpec((1, g, hd, Q_BLOCK), lambda h, b: (h, 0, 0, b + first_tile)),
                  pl.BlockSpec((1, tc, hd), lambda h, b: (h, 0, 0)),
                  pl.BlockSpec((1, hv, tc), lambda h, b: (h, 0, 0)),
                  kspec(prev), kspec(cur), kspec(nxt), vspec(prev), vspec(cur), vspec(nxt),
                  pl.BlockSpec((1, nkeys, m_lanes), lambda h, b: (variant(b), 0, 0)),
                  pl.BlockSpec((1, 1, m_lanes), lambda h, b: (h, 0, 0))],
        out_specs=pl.BlockSpec((1, g, hd, Q_BLOCK), lambda h, b: (h, 0, 0, b)),
        out_shape=jax.ShapeDtypeStruct((kvh, g, hd, seq), MXU_DTYPE),
        compiler_params=_params("parallel", "parallel"),
        name="window_attn",
    )(qt, kc, vc, kl, kl, kl, vl, vl, vl, _window_bias(tc, g), sink_m)


def _rope_angles(seq, hd):
    d_axis = hd // 2
    inv_freq = ROPE_THETA ** (-jnp.arange(0, d_axis, 2, dtype=jnp.float32) / d_axis)
    t = jnp.arange(seq)
    row = (t // GRID_W).astype(jnp.float32)[:, None] * inv_freq
    col = (t % GRID_W).astype(jnp.float32)[:, None] * inv_freq
    return jnp.concatenate([row, col], axis=1)


def _rope_tables_t(seq, tc, pad, hd):
    ang = _rope_angles(seq, hd).T
    cos = jnp.concatenate([jnp.ones((hd // 2, tc), jnp.float32), jnp.cos(ang),
                           jnp.ones((hd // 2, pad), jnp.float32)], axis=1)
    sin = jnp.concatenate([jnp.zeros((hd // 2, tc), jnp.float32), jnp.sin(ang),
                           jnp.zeros((hd // 2, pad), jnp.float32)], axis=1)
    return cos, sin


def _rope_tables(seq, tc, pad, hd):
    ang = _rope_angles(seq, hd)
    cos = jnp.concatenate([jnp.cos(ang), jnp.cos(ang)], axis=1)
    sin = jnp.concatenate([-jnp.sin(ang), jnp.sin(ang)], axis=1)
    reps = V7X_LANES // hd
    cos = jnp.tile(cos, (1, reps))
    sin = jnp.tile(sin, (1, reps))
    cos = jnp.concatenate([jnp.ones((tc, V7X_LANES), jnp.float32), cos,
                           jnp.ones((pad, V7X_LANES), jnp.float32)], axis=0)
    sin = jnp.concatenate([jnp.zeros((tc, V7X_LANES), jnp.float32), sin,
                           jnp.zeros((pad, V7X_LANES), jnp.float32)], axis=0)
    return cos, sin


def _head_perm(hd):
    f = hd // 4
    idx = np.arange(hd).reshape(2, 2, f)
    return idx.transpose(1, 0, 2).reshape(hd)


def _v_aug(vt):
    kvh, _, n = vt.shape
    ones = jnp.ones((kvh, 1, n), vt.dtype)
    zeros = jnp.zeros((kvh, V7X_BF16_ROWS - 1, n), vt.dtype)
    return jnp.concatenate([vt, ones, zeros], axis=1)


def _attention_layer(xs, h, ht, gates, w_qkv, w_o, q_norm, k_norm, sink, hd, g, tc, seq, windowed, with_ctx_out):
    t_pad, d = h.shape
    t = tc + seq
    nheads = d // hd
    kvh = nheads // g
    qd, kd = nheads * hd, kvh * hd
    perm = _head_perm(hd)
    wqt = w_qkv[:, :qd].T.reshape(nheads, hd, d)[:, perm].reshape(qd, d).astype(MXU_DTYPE)
    wk = w_qkv[:, qd:qd + kd].reshape(d, kvh, hd)[:, :, perm].reshape(d, kd).astype(MXU_DTYPE)
    wvt = w_qkv[:, qd + kd:].T.astype(MXU_DTYPE)
    qscale = (hd ** -0.5) * LOG2E
    nwq = jnp.tile(q_norm[perm] * qscale, nheads)
    nwk = jnp.tile(k_norm[perm], kvh).reshape(1, kd)
    cos, sin = _rope_tables(seq, tc, t_pad - t, hd)
    cos_t, sin_t = _rope_tables_t(seq, tc, t_pad - t, hd)
    qt = _mm_qk_t(wqt, ht, nwq, cos_t, sin_t, hd, name="proj_q").reshape(kvh, g, hd, t_pad)
    k = _mm_qk(h, wk, nwk, cos, sin, hd, name="proj_k")[:t]
    vt = _mm(wvt, ht, MXU_DTYPE, tm_prefs=(512, 256, 128), tn_prefs=TOKEN_TILES, name="proj_v")

    m_lanes = 1024
    tq = m_lanes // g
    assert tc % tq == 0 and seq % tq == 0 and (not windowed or tq == Q_BLOCK)
    kk = k.reshape(t, kvh, hd).transpose(1, 0, 2)
    vt = vt.reshape(kvh, hd, t_pad)[:, :, :t]
    if sink is None:
        m_init = jnp.full((kvh, 1, m_lanes), NEG_INF, jnp.float32)
        init_l = 0.0
    else:
        m_init = jnp.repeat(sink.reshape(kvh, g).astype(jnp.float32) * LOG2E, tq, axis=1).reshape(kvh, 1, m_lanes)
        init_l = 1.0
    kc, vc = kk[:, :tc], vt[:, :, :tc]
    if windowed:
        va = _v_aug(vt)
        o_lat = _window(qt, kc, va[:, :, :tc], kk[:, tc:], va[:, :, tc:], m_init, tc // tq, tc, seq)
    else:
        tk = _pick(seq, FLASH_KEY_CHUNKS)
        kl = kk[:, tc:].reshape(kvh, seq // tk, tk, hd)
        vl = vt[:, :, tc:].reshape(kvh, hd, seq // tk, tk).transpose(0, 2, 1, 3)
        o_lat = _flash(qt, kc, vc, kl, vl, m_init, init_l, tc // tq, seq // tq, name="global_attn")
    if with_ctx_out:
        o_ctx = _flash(qt, kc, vc, None, None, m_init, init_l, 0, tc // tq, name="ctx_attn")
    else:
        o_ctx = jnp.zeros((kvh, g, hd, tc), MXU_DTYPE)
    o_pad = jnp.zeros((kvh, g, hd, t_pad - t), MXU_DTYPE)
    ot = jnp.concatenate([o_ctx, o_lat, o_pad], axis=3).reshape(qd, t_pad)
    return _mm_res_t(w_o.T.astype(MXU_DTYPE), ot, xs, gates, tc, name="mixer_out_t")


def _dft_parts(n, scale):
    k = np.arange(n)
    ang = 2.0 * np.pi * ((k[:, None] * k[None, :]) % n) / n
    return np.cos(ang) * scale, -np.sin(ang) * scale


def _chan_dft_kernel(x_ref, c_ref, s_ref, o_ref):
    x = x_ref[...]
    o_ref[0] = jnp.dot(x, c_ref[...], preferred_element_type=jnp.float32).astype(o_ref.dtype)
    o_ref[1] = jnp.dot(x, s_ref[...], preferred_element_type=jnp.float32).astype(o_ref.dtype)


def _chan_dft(h, gc):
    t, d = h.shape
    re, im = _dft_parts(gc, gc ** -0.5)
    tm = _pick(t, ROW_TILES)
    return pl.pallas_call(
        _chan_dft_kernel,
        grid=(t // tm, d // gc),
        in_specs=[pl.BlockSpec((tm, gc), lambda i, j: (i, j)),
                  pl.BlockSpec((gc, gc), lambda i, j: (0, 0)),
                  pl.BlockSpec((gc, gc), lambda i, j: (0, 0))],
        out_specs=pl.BlockSpec((2, tm, gc), lambda i, j: (0, i, j)),
        out_shape=jax.ShapeDtypeStruct((2, t, d), MXU_DTYPE),
        compiler_params=_params("parallel", "parallel"),
        name="chan_dft",
    )(h, jnp.asarray(re, MXU_DTYPE), jnp.asarray(im, MXU_DTYPE))


def _pos_stage1_kernel(m_ref, y_ref, tr_ref, ti_ref, o_ref, *, n1):
    y = jnp.concatenate([y_ref[0], y_ref[1]], axis=0)
    a = jnp.dot(m_ref[...], y, preferred_element_type=jnp.float32)
    ar, ai = a[:n1], a[n1:]
    tr, ti = tr_ref[0][:, 0:1], ti_ref[0][:, 0:1]
    o_ref[0] = (ar * tr - ai * ti).astype(o_ref.dtype)
    o_ref[1] = (ar * ti + ai * tr).astype(o_ref.dtype)


def _pos_stage2_kernel(m_ref, b_ref, o_ref):
    b = jnp.concatenate([b_ref[0, 0], b_ref[1, 0]], axis=0)
    o_ref[0] = jnp.dot(m_ref[...], b, preferred_element_type=jnp.float32).astype(o_ref.dtype)


def _pos_dft_two_stage(y, n1, n2):
    _, s, d = y.shape
    sc = float(s) ** -0.25
    r1, i1 = _dft_parts(n1, sc)
    m1 = np.block([[r1, -i1], [i1, r1]])
    f1 = np.arange(n1)[None, :]
    t2 = np.arange(n2)[:, None]
    ang = 2.0 * np.pi * (t2 * f1) / s
    tw_r = np.broadcast_to(np.cos(ang)[:, :, None], (n2, n1, V7X_LANES))
    tw_i = np.broadcast_to(-np.sin(ang)[:, :, None], (n2, n1, V7X_LANES))
    y4 = y.reshape(2, n1, n2 * d)
    b = pl.pallas_call(
        functools.partial(_pos_stage1_kernel, n1=n1),
        grid=(n2,),
        in_specs=[pl.BlockSpec((2 * n1, 2 * n1), lambda j: (0, 0)),
                  pl.BlockSpec((2, n1, d), lambda j: (0, 0, j)),
                  pl.BlockSpec((1, n1, V7X_LANES), lambda j: (j, 0, 0)),
                  pl.BlockSpec((1, n1, V7X_LANES), lambda j: (j, 0, 0))],
        out_specs=pl.BlockSpec((2, n1, d), lambda j: (0, 0, j)),
        out_shape=jax.ShapeDtypeStruct((2, n1, n2 * d), MXU_DTYPE),
        compiler_params=_params("parallel"),
        name="pos_dft_stage1",
    )(jnp.asarray(m1, MXU_DTYPE), y4, jnp.asarray(tw_r, jnp.float32), jnp.asarray(tw_i, jnp.float32))
    b4 = b.reshape(2, n1, n2, d)
    r2, i2 = _dft_parts(n2, sc)
    m2 = np.concatenate([r2, -i2], axis=1)
    tcn = _pick(d, (2048, 1024, 512, 256, 128))
    out = pl.pallas_call(
        _pos_stage2_kernel,
        grid=(n1, d // tcn),
        in_specs=[pl.BlockSpec((n2, 2 * n2), lambda f, j: (0, 0)),
                  pl.BlockSpec((2, 1, n2, tcn), lambda f, j: (0, f, 0, j))],
        out_specs=pl.BlockSpec((1, n2, tcn), lambda f, j: (f, 0, j)),
        out_shape=jax.ShapeDtypeStruct((n1, n2, d), MXU_DTYPE),
        compiler_params=_params("parallel", "parallel"),
        name="pos_dft_stage2",
    )(jnp.asarray(m2, MXU_DTYPE), b4)
    return out.transpose(1, 0, 2).reshape(s, d)


def _fourier_mix(h, tc, seq):
    t_pad, d = h.shape
    y = _chan_dft(h, d // FNET_GROUPS)
    rc, ic = _dft_parts(tc, tc ** -0.5)
    mc = np.concatenate([rc, -ic], axis=1)
    yc = jnp.concatenate([y[0, :tc], y[1, :tc]], axis=0)
    mixed_c = _mm(jnp.asarray(mc, MXU_DTYPE), yc, MXU_DTYPE, name="ctx_pos_dft")
    n1 = 1 << (int(math.log2(seq)) // 2)
    n2 = seq // n1
    mixed_l = _pos_dft_two_stage(y[:, tc:tc + seq], n1, n2)
    return jnp.concatenate([mixed_c, mixed_l, jnp.zeros((t_pad - tc - seq, d), MXU_DTYPE)], axis=0)


def _top16(s):
    n = s.shape[0]
    rows = lax.broadcasted_iota(jnp.int32, s.shape, 0)
    row16 = lax.broadcasted_iota(jnp.int32, (PEER_TOPK, s.shape[1]), 0)
    rank = jnp.full(s.shape, RANK_NONE, jnp.float32)
    sv = jnp.zeros((PEER_TOPK, s.shape[1]), jnp.float32)
    v = s
    for k in range(PEER_TOPK):
        m = jnp.max(v, axis=0, keepdims=True)
        idx = jnp.min(jnp.where(v == m, rows, n), axis=0, keepdims=True)
        hit = rows == idx
        rank = jnp.where(hit, float(k), rank)
        v = jnp.where(hit, -jnp.inf, v)
        sv = jnp.where(row16 == k, m, sv)
    return rank, sv


_CAND_GROUPS = ((0, 0, 8), (0, 8, 8), (1, 0, 8), (2, 0, 5), (3, 0, 4), (4, 0, 3), (5, 0, 2), (6, 0, 2), (7, 0, 2))


def _select_pairs(sv0, sv1):
    lanes = sv0.shape[1]
    sub = lax.broadcasted_iota(jnp.int32, (V7X_SUBLANES, lanes), 0)
    cands, poss = [], []
    for r0, r1_0, cnt in _CAND_GROUPS:
        c = sv0[r0:r0 + 1] + sv1[r1_0:r1_0 + V7X_SUBLANES]
        cands.append(jnp.where(sub < cnt, c, -jnp.inf))
        poss.append(r0 * PEER_TOPK + r1_0 + sub)
    cands.append(sv0[V7X_SUBLANES:] + sv1[0:1])
    poss.append((V7X_SUBLANES + sub) * PEER_TOPK)
    cand = jnp.concatenate(cands, axis=0)
    pos = jnp.concatenate(poss, axis=0)
    cmax = sv0[0:1] + sv1[0:1]
    big = PEER_TOPK * PEER_TOPK
    c = cand
    sel = jnp.zeros(cand.shape, jnp.float32)
    for _ in range(PEER_TOPK):
        m = jnp.max(c, axis=0, keepdims=True)
        pidx = jnp.min(jnp.where(c == m, pos, big), axis=0, keepdims=True)
        hit = pos == pidx
        sel = jnp.where(hit, 1.0, sel)
        c = jnp.where(hit, -jnp.inf, c)
    z = jnp.sum(jnp.where(sel > 0.0, jnp.exp(cand - cmax), 0.0), axis=0, keepdims=True)
    g = V7X_SUBLANES
    rows = [jnp.sum(sel[0:2 * g], axis=0, keepdims=True)]
    for i in range(2, len(_CAND_GROUPS)):
        rows.append(jnp.sum(sel[i * g:(i + 1) * g], axis=0, keepdims=True))
    rows.append(sel[len(_CAND_GROUPS) * g:])
    return jnp.concatenate(rows, axis=0), z


def _peer_select_kernel(q_ref, keys_ref, r1_ref, b_ref, cnt_ref, a_ref):
    def head(hh, carry):
        base = pl.multiple_of(hh * 2 * PEER_HALF, 2 * PEER_HALF)
        q0 = q_ref[pl.ds(base, PEER_HALF), :]
        q1 = q_ref[pl.ds(base + PEER_HALF, PEER_HALF), :]
        s0 = jnp.dot(keys_ref[hh, 0], q0, preferred_element_type=jnp.float32, precision=lax.Precision.HIGHEST)
        s1 = jnp.dot(keys_ref[hh, 1], q1, preferred_element_type=jnp.float32, precision=lax.Precision.HIGHEST)
        rank0, sv0 = _top16(s0)
        rank1, sv1 = _top16(s1)
        cnt16, z = _select_pairs(sv0, sv1)
        cnt = jnp.zeros(s0.shape, jnp.float32)
        for r in range(PEER_TOPK):
            cnt = jnp.where(rank0 == float(r), cnt16[r:r + 1], cnt)
        r1_ref[hh] = rank1
        b_ref[hh] = jnp.exp(s1 - sv1[0:1])
        cnt_ref[hh] = cnt
        a_ref[hh] = jnp.exp(s0 - sv0[0:1]) / z
        return carry

    lax.fori_loop(0, PEER_HEADS, head, 0, unroll=4)


def _peer_select(qt, sub_keys):
    _, t = qt.shape
    tl = _pick(t, (2 * V7X_LANES, V7X_LANES))
    shp = jax.ShapeDtypeStruct((PEER_HEADS, PEER_N_KEYS, t), jnp.float32)
    ospec = pl.BlockSpec((PEER_HEADS, PEER_N_KEYS, tl), lambda i: (0, 0, i))
    return pl.pallas_call(
        _peer_select_kernel,
        grid=(t // tl,),
        in_specs=[pl.BlockSpec((qt.shape[0], tl), lambda i: (0, i)),
                  pl.BlockSpec(sub_keys.shape, lambda i: (0, 0, 0, 0))],
        out_specs=[ospec] * 4,
        out_shape=[shp] * 4,
        compiler_params=_params("parallel"),
        name="peer_select",
    )(qt, sub_keys)


def _gelu_exact(x):
    return 0.5 * x * (1.0 + lax.erf(x * (2.0 ** -0.5)))


def _peer_act_kernel(u_ref, h_ref, r1_ref, b_ref, cnt_ref, a_ref, o_ref, *, iblocks):
    nk = PEER_N_KEYS
    groups = [slice(g * MXU_COLS, (g + 1) * MXU_COLS) for g in range(h_ref.shape[1] // MXU_COLS)]
    pres = [jnp.dot(u_ref[...], h_ref[:, cols], preferred_element_type=jnp.float32) for cols in groups]
    dt = o_ref.dtype
    for pre, cols in zip(pres, groups):
        r1 = [r1_ref[hh, :, cols].astype(dt) for hh in range(PEER_HEADS)]
        b = [b_ref[hh, :, cols].astype(dt) for hh in range(PEER_HEADS)]
        for ii in range(iblocks):
            w = jnp.zeros((nk, MXU_COLS), dt)
            for hh in range(PEER_HEADS):
                sel = r1[hh] < cnt_ref[hh, ii:ii + 1, cols].astype(dt)
                w = w + jnp.where(sel, b[hh], jnp.zeros_like(b[hh])) * a_ref[hh, ii:ii + 1, cols].astype(dt)
            o_ref[ii * nk:(ii + 1) * nk, cols] = _gelu_exact(pre[ii * nk:(ii + 1) * nk]).astype(dt) * w


def _peer_act(u, layer, ht, r1, b, cnt, a):
    _, e, d = u.shape
    _, t = ht.shape
    iblocks = V7X_SUBLANES
    te = iblocks * PEER_N_KEYS
    tt = _pick(t, TOKEN_TILES)
    full = pl.BlockSpec((PEER_HEADS, PEER_N_KEYS, tt), lambda i, j: (0, 0, i))
    part = pl.BlockSpec((PEER_HEADS, iblocks, tt), lambda i, j: (0, j, i))
    return pl.pallas_call(
        functools.partial(_peer_act_kernel, iblocks=iblocks),
        grid=(t // tt, e // te),
        in_specs=[pl.BlockSpec((None, te, d), lambda i, j: (layer, j, 0)),
                  pl.BlockSpec((d, tt), lambda i, j: (0, i)),
                  full, full, part, part],
        out_specs=pl.BlockSpec((te, tt), lambda i, j: (j, i)),
        out_shape=jax.ShapeDtypeStruct((e, t), MXU_DTYPE),
        compiler_params=_params("parallel", "parallel"),
        name="peer_act",
    )(u, ht, r1, b, cnt, a)


def _peer(x, nw, mod_ctx, mod_lat, gates, tc, w_q, sub_keys, u_all, vt_all, layer):
    (ht,) = _modulate(x, nw, mod_ctx, mod_lat, tc, ("feature_major",))
    qt = _mm(w_q.T.astype(MXU_DTYPE), ht, jnp.float32, tm_prefs=(512, 256, 128),
             tn_prefs=TOKEN_TILES, name="peer_query")
    r1, b, cnt, a = _peer_select(qt, sub_keys)
    act = _peer_act(u_all, layer, ht, r1, b, cnt, a)
    return _mm_res_t(vt_all, act, x, gates, tc, layer=layer, name="peer_out")


def kernel(x, c, ctx, c_ctx, ada_w, ada_b, mix_norm, ffn_norm, a_w_qkv, a_w_o, a_q_norm, a_k_norm,
           b_w_qkv, b_w_o, b_q_norm, b_k_norm, b_sink, f_w_out, peer_w_q, peer_sub_keys, peer_u, peer_v):
    batch, seq, d = x.shape
    assert batch == 1 and ctx.shape[0] == 1
    tc = ctx.shape[1]
    depth = ada_w.shape[0]
    pad = -(tc + seq) % TOKEN_TILES[0]
    xs = jnp.concatenate([ctx[0], x[0], jnp.zeros((pad, d), x.dtype)], axis=0)
    mods = _ada(jnp.stack([c[0], c_ctx]), ada_w, ada_b)
    mods = mods.reshape(depth, 2, N_MOD, d)
    u_all = peer_u.astype(MXU_DTYPE)
    vt_all = peer_v.transpose(0, 2, 1).astype(MXU_DTYPE)
    for layer in range(depth):
        last = layer == depth - 1
        kind, j = layer % N_MIXERS, layer // N_MIXERS
        lat, cx = mods[layer, 0], mods[layer, 1]
        gates = jnp.stack([cx[2], lat[2]])
        if kind == 2:
            (h,) = _modulate(xs, mix_norm[layer], cx[0:2], lat[0:2], tc, ("token_major",))
            xs = _mm_res(_fourier_mix(h, tc, seq), f_w_out[j].astype(MXU_DTYPE), xs, gates, tc, name="mixer_out")
        else:
            h, ht = _modulate(xs, mix_norm[layer], cx[0:2], lat[0:2], tc, ("token_major", "feature_major"))
            if kind == 0:
                xs = _attention_layer(xs, h, ht, gates, a_w_qkv[j], a_w_o[j], a_q_norm[j], a_k_norm[j], None,
                                      A_HEAD_DIM, A_GROUP, tc, seq, False, not last)
            else:
                xs = _attention_layer(xs, h, ht, gates, b_w_qkv[j], b_w_o[j], b_q_norm[j], b_k_norm[j], b_sink[j],
                                      B_HEAD_DIM, B_GROUP, tc, seq, True, not last)
        xs = _peer(xs, ffn_norm[layer], cx[3:5], lat[3:5], jnp.stack([cx[5], lat[5]]), tc,
                   peer_w_q[layer], peer_sub_keys[layer], u_all, vt_all, layer)
    return xs[tc:tc + seq][None]
```

```python
import functools
import math

import numpy as np
import jax
import jax.numpy as jnp
from jax import lax
from jax.experimental import pallas as pl
from jax.experimental.pallas import tpu as pltpu

GRID_W = 64
Q_BLOCK = 128
ROPE_THETA = 10000.0
NORM_EPS = 1e-6
NEG_INF = -1e30
N_MOD = 6
N_MIXERS = 3
A_HEAD_DIM = 128
A_GROUP = 4
B_HEAD_DIM = 64
B_GROUP = 8
WINDOW = 128
FNET_GROUPS = 8
PEER_HEADS = 8
PEER_N_KEYS = 96
PEER_TOPK = 16
PEER_HALF = 128

V7X_LANES = 128
V7X_SUBLANES = 8
V7X_BF16_ROWS = 16
MXU_COLS = 256
V7X_VMEM_BYTES = 64 * 1024 * 1024
VMEM_LIMIT = V7X_VMEM_BYTES - 8 * 1024 * 1024

MXU_DTYPE = jnp.bfloat16
LOG2E = 1.4426950408889634
RANK_NONE = 99.0
ROW_TILES = (1536, 768, 512, 256, 128)
TOKEN_TILES = (768, 512, 256)
FLASH_KEY_CHUNKS = (4096, 2048, 1024, 512, 256)
EXP2_HEADROOM = 64.0


def _params(*sem):
    return pltpu.CompilerParams(dimension_semantics=sem, vmem_limit_bytes=VMEM_LIMIT)


def _pick(n, prefs):
    for p in prefs:
        if n % p == 0:
            return p
    return n


def _ada_kernel(cond_ref, w_ref, b_ref, o_ref):
    w = w_ref[0]
    reps = w.shape[1] // V7X_LANES
    rows = []
    for r in range(2):
        c = cond_ref[r]
        s = c * jax.nn.sigmoid(c)
        rows.append(jnp.sum(w * jnp.tile(s, (1, reps)), axis=0, keepdims=True))
    o_ref[0] = jnp.concatenate(rows, axis=0) + b_ref[0]


def _ada(cond2, ada_w, ada_b):
    depth, d, n = ada_w.shape
    tn = _pick(n, (512, 256, 128))
    cond_rep = jnp.broadcast_to(cond2[:, :, None], (2, d, V7X_LANES))
    return pl.pallas_call(
        _ada_kernel,
        grid=(depth, n // tn),
        in_specs=[pl.BlockSpec((2, d, V7X_LANES), lambda l, j: (0, 0, 0)),
                  pl.BlockSpec((1, d, tn), lambda l, j: (l, 0, j)),
                  pl.BlockSpec((1, 1, tn), lambda l, j: (l, 0, j))],
        out_specs=pl.BlockSpec((1, 2, tn), lambda l, j: (l, 0, j)),
        out_shape=jax.ShapeDtypeStruct((depth, 2, n), jnp.float32),
        compiler_params=_params("parallel", "parallel"),
        name="ada",
    )(cond_rep, ada_w, ada_b.reshape(depth, 1, n))


def _modulate_kernel(x_ref, nw_ref, mod_ref, *o_refs, layouts):
    x = x_ref[...]
    y = x * lax.rsqrt(jnp.mean(x * x, axis=-1, keepdims=True) + NORM_EPS) * nw_ref[...]
    shift = mod_ref[0, 0:1, :]
    scale = mod_ref[0, 1:2, :]
    h = y * (1.0 + scale) + shift
    for o_ref, layout in zip(o_refs, layouts):
        o_ref[...] = (h.T if layout == "feature_major" else h).astype(o_ref.dtype)


def _modulate(x, nw, mod_ctx, mod_lat, tc, layouts):
    t, d = x.shape
    tm = _pick(math.gcd(t, tc), (256, 128))
    ctx_tiles = tc // tm
    mods = jnp.stack([mod_ctx, mod_lat])
    shapes = [jax.ShapeDtypeStruct((d, t) if lay == "feature_major" else (t, d), MXU_DTYPE) for lay in layouts]
    specs = [pl.BlockSpec((d, tm), lambda i: (0, i)) if lay == "feature_major"
             else pl.BlockSpec((tm, d), lambda i: (i, 0)) for lay in layouts]
    return pl.pallas_call(
        functools.partial(_modulate_kernel, layouts=tuple(layouts)),
        grid=(t // tm,),
        in_specs=[pl.BlockSpec((tm, d), lambda i: (i, 0)),
                  pl.BlockSpec((1, d), lambda i: (0, 0)),
                  pl.BlockSpec((1, 2, d), lambda i: (jnp.where(i < ctx_tiles, 0, 1), 0, 0))],
        out_specs=specs,
        out_shape=shapes,
        compiler_params=_params("parallel"),
        name="modulate",
    )(x, nw.reshape(1, d), mods)


def _mm_kernel(a_ref, b_ref, o_ref):
    o_ref[...] = jnp.dot(a_ref[...], b_ref[...], preferred_element_type=jnp.float32).astype(o_ref.dtype)


def _mm(a, b, out_dtype, *, tm_prefs=ROW_TILES, tn_prefs=(512, 256, 128), name="mm"):
    m, k = a.shape
    _, n = b.shape
    tm, tn = _pick(m, tm_prefs), _pick(n, tn_prefs)
    return pl.pallas_call(
        _mm_kernel,
        grid=(m // tm, n // tn),
        in_specs=[pl.BlockSpec((tm, k), lambda i, j: (i, 0)),
                  pl.BlockSpec((k, tn), lambda i, j: (0, j))],
        out_specs=pl.BlockSpec((tm, tn), lambda i, j: (i, j)),
        out_shape=jax.ShapeDtypeStruct((m, n), out_dtype),
        compiler_params=_params("parallel", "parallel"),
        name=name,
    )(a, b)


def _mm_res_kernel(a_ref, b_ref, res_ref, g_ref, o_ref, *, tc, transposed):
    acc = jnp.dot(a_ref[...], b_ref[...], preferred_element_type=jnp.float32)
    if transposed:
        acc = acc.T
        row0 = pl.program_id(0) * acc.shape[0]
    else:
        row0 = pl.program_id(0) * acc.shape[0]
    rid = row0 + lax.broadcasted_iota(jnp.int32, acc.shape, 0)
    gate = jnp.where(rid < tc, g_ref[0:1, :], g_ref[1:2, :])
    o_ref[...] = res_ref[...] + gate * acc


def _mm_res(a, b, res, gates, tc, *, name="mm_res"):
    m, k = a.shape
    _, n = b.shape
    tm, tn = _pick(m, ROW_TILES), _pick(n, (512, 256, 128))
    return pl.pallas_call(
        functools.partial(_mm_res_kernel, tc=tc, transposed=False),
        grid=(m // tm, n // tn),
        in_specs=[pl.BlockSpec((tm, k), lambda i, j: (i, 0)),
                  pl.BlockSpec((k, tn), lambda i, j: (0, j)),
                  pl.BlockSpec((tm, tn), lambda i, j: (i, j)),
                  pl.BlockSpec((2, tn), lambda i, j: (0, j))],
        out_specs=pl.BlockSpec((tm, tn), lambda i, j: (i, j)),
        out_shape=jax.ShapeDtypeStruct((m, n), jnp.float32),
        compiler_params=_params("parallel", "parallel"),
        name=name,
    )(a, b, res, gates)


def _mm_res_t(vt, at, res, gates, tc, *, layer=None, name="mm_res_t"):
    n, k = vt.shape[-2:]
    _, t = at.shape
    vspec = (lambda tn: pl.BlockSpec((tn, k), lambda i, j: (j, 0))) if layer is None else (
        lambda tn: pl.BlockSpec((None, tn, k), lambda i, j: (layer, j, 0)))
    tt, tn = _pick(t, TOKEN_TILES), _pick(n, (512, 256, 128) if k <= 4096 else (256, 128))
    return pl.pallas_call(
        functools.partial(_mm_res_kernel, tc=tc, transposed=True),
        grid=(t // tt, n // tn),
        in_specs=[vspec(tn),
                  pl.BlockSpec((k, tt), lambda i, j: (0, i)),
                  pl.BlockSpec((tt, tn), lambda i, j: (i, j)),
                  pl.BlockSpec((2, tn), lambda i, j: (0, j))],
        out_specs=pl.BlockSpec((tt, tn), lambda i, j: (i, j)),
        out_shape=jax.ShapeDtypeStruct((t, n), jnp.float32),
        compiler_params=_params("parallel", "parallel"),
        name=name,
    )(vt, at, res, gates)


def _mm_qk_kernel(a_ref, b_ref, nw_ref, cos_ref, sin_ref, o_ref, *, hd):
    gw = min(MXU_COLS, b_ref.shape[1])
    accs = [jnp.dot(a_ref[...], b_ref[:, g * gw:(g + 1) * gw], preferred_element_type=jnp.float32)
            for g in range(b_ref.shape[1] // gw)]
    cos = cos_ref[...]
    sin = sin_ref[...]
    lane = lax.broadcasted_iota(jnp.int32, (a_ref.shape[0], V7X_LANES), 1)
    per_group = gw // V7X_LANES
    for c in range(b_ref.shape[1] // V7X_LANES):
        sub = c % per_group
        blk = accs[c // per_group][:, sub * V7X_LANES:(sub + 1) * V7X_LANES]
        sq = blk * blk
        if hd == V7X_LANES:
            ms = jnp.sum(sq, axis=-1, keepdims=True) * (1.0 / hd)
        else:
            lo = jnp.sum(jnp.where(lane < hd, sq, 0.0), axis=-1, keepdims=True)
            hi = jnp.sum(jnp.where(lane < hd, 0.0, sq), axis=-1, keepdims=True)
            ms = jnp.where(lane < hd, lo, hi) * (1.0 / hd)
        y = blk * lax.rsqrt(ms + NORM_EPS) * nw_ref[:, c * V7X_LANES:(c + 1) * V7X_LANES]
        if hd == V7X_LANES:
            partner = pltpu.roll(y, hd // 2, 1)
        else:
            partner = jnp.where(lane % hd < hd // 2,
                                pltpu.roll(y, V7X_LANES - hd // 2, 1), pltpu.roll(y, hd // 2, 1))
        o_ref[:, c * V7X_LANES:(c + 1) * V7X_LANES] = (y * cos + partner * sin).astype(o_ref.dtype)


def _mm_qk(a, b, nw, cos, sin, hd, *, name):
    m, k = a.shape
    _, n = b.shape
    tm, tn = _pick(m, ROW_TILES), _pick(n, (512, 256, 128))
    return pl.pallas_call(
        functools.partial(_mm_qk_kernel, hd=hd),
        grid=(m // tm, n // tn),
        in_specs=[pl.BlockSpec((tm, k), lambda i, j: (i, 0)),
                  pl.BlockSpec((k, tn), lambda i, j: (0, j)),
                  pl.BlockSpec((1, tn), lambda i, j: (0, j)),
                  pl.BlockSpec((tm, V7X_LANES), lambda i, j: (i, 0)),
                  pl.BlockSpec((tm, V7X_LANES), lambda i, j: (i, 0))],
        out_specs=pl.BlockSpec((tm, tn), lambda i, j: (i, j)),
        out_shape=jax.ShapeDtypeStruct((m, n), MXU_DTYPE),
        compiler_params=_params("parallel", "parallel"),
        name=name,
    )(a, b, nw, cos, sin)


def _mm_qk_t_kernel(w_ref, h_ref, nw_ref, cos_ref, sin_ref, o_ref, *, hd):
    acc = jnp.dot(w_ref[...], h_ref[...], preferred_element_type=jnp.float32)
    cos = cos_ref[...]
    sin = sin_ref[...]
    reps = acc.shape[1] // V7X_LANES
    half = hd // 2
    for r in range(acc.shape[0] // hd):
        x = acc[r * hd:(r + 1) * hd]
        ms = jnp.sum(x * x, axis=0, keepdims=True) * (1.0 / hd)
        y = x * lax.rsqrt(ms + NORM_EPS) * jnp.tile(nw_ref[r * hd:(r + 1) * hd], (1, reps))
        y1, y2 = y[:half], y[half:]
        o_ref[r * hd:r * hd + half] = (y1 * cos - y2 * sin).astype(o_ref.dtype)
        o_ref[r * hd + half:(r + 1) * hd] = (y1 * sin + y2 * cos).astype(o_ref.dtype)


def _mm_qk_t(wt, ht, nw, cos, sin, hd, *, name):
    n, k = wt.shape
    _, t = ht.shape
    tf, tt = _pick(n, (512, 256, 128)), _pick(t, TOKEN_TILES)
    nw_rep = jnp.broadcast_to(nw[:, None], (n, V7X_LANES))
    return pl.pallas_call(
        functools.partial(_mm_qk_t_kernel, hd=hd),
        grid=(t // tt, n // tf),
        in_specs=[pl.BlockSpec((tf, k), lambda i, j: (j, 0)),
                  pl.BlockSpec((k, tt), lambda i, j: (0, i)),
                  pl.BlockSpec((tf, V7X_LANES), lambda i, j: (j, 0)),
                  pl.BlockSpec((hd // 2, tt), lambda i, j: (0, i)),
                  pl.BlockSpec((hd // 2, tt), lambda i, j: (0, i))],
        out_specs=pl.BlockSpec((tf, tt), lambda i, j: (j, i)),
        out_shape=jax.ShapeDtypeStruct((n, t), MXU_DTYPE),
        compiler_params=_params("parallel", "parallel"),
        name=name,
    )(wt, ht, nw_rep, cos, sin)


def _flash_kernel(q_ref, kc_ref, vc_ref, *rest, init_l, n_lat):
    if n_lat:
        kl_ref, vl_ref, m0_ref, o_ref, mrun_ref, macc_ref, l_ref, acc_ref, knorm_ref = rest
    else:
        m0_ref, o_ref, mrun_ref, macc_ref, l_ref, acc_ref, knorm_ref = rest
    q = jnp.concatenate([q_ref[0, g] for g in range(q_ref.shape[1])], axis=1)

    def norm2(k):
        kf = k.astype(jnp.float32)
        return jnp.max(jnp.sum(kf * kf, axis=1, keepdims=True), axis=0, keepdims=True)

    if n_lat:
        @pl.when(pl.program_id(1) == 0)
        def _():
            k2 = lax.fori_loop(0, n_lat, lambda c, mx: jnp.maximum(mx, norm2(kl_ref[0, c])), norm2(kc_ref[0]))
            knorm_ref[...] = jnp.broadcast_to(jnp.sqrt(k2), knorm_ref.shape)

    mrun_ref[...] = m0_ref[0]
    macc_ref[...] = m0_ref[0]
    l_ref[...] = jnp.full(l_ref.shape, init_l, jnp.float32)
    acc_ref[...] = jnp.zeros(acc_ref.shape, jnp.float32)

    def accumulate(p, v, m_to):
        alpha = jnp.exp2(macc_ref[...] - m_to)
        l_ref[...] = l_ref[...] * alpha + jnp.sum(p, axis=0, keepdims=True)
        acc_ref[...] = acc_ref[...] * alpha + jnp.dot(v, p.astype(MXU_DTYPE), preferred_element_type=jnp.float32)
        macc_ref[...] = m_to

    def own_max(k, v):
        s = jnp.dot(k, q, preferred_element_type=jnp.float32)
        m_new = jnp.maximum(mrun_ref[...], jnp.max(s, axis=0, keepdims=True))
        accumulate(jnp.exp2(s - m_new), v, m_new)
        mrun_ref[...] = m_new

    def earlier_max(k, v):
        m_use = mrun_ref[...]
        s = jnp.dot(k, q, preferred_element_type=jnp.float32)
        accumulate(jnp.exp2(s - m_use), v, m_use)
        mrun_ref[...] = jnp.maximum(m_use, jnp.max(s, axis=0, keepdims=True))

    own_max(kc_ref[0], vc_ref[0])
    if n_lat:
        qf = q.astype(jnp.float32)
        qnorm = jnp.sqrt(jnp.sum(qf * qf, axis=0, keepdims=True))
        bounded = jnp.max(qnorm * knorm_ref[...] - mrun_ref[...]) <= EXP2_HEADROOM

        def loop(step):
            def body(c, carry):
                step(kl_ref[0, c], vl_ref[0, c])
                return carry
            lax.fori_loop(0, n_lat, body, 0)

        pl.when(bounded)(lambda: loop(earlier_max))
        pl.when(jnp.logical_not(bounded))(lambda: loop(own_max))

    o = (acc_ref[...] / l_ref[...]).astype(o_ref.dtype)
    tq = o_ref.shape[3]
    for g in range(o_ref.shape[1]):
        o_ref[0, g] = o[:, g * tq:(g + 1) * tq]


def _flash(qt, kc, vc, kl, vl, m0, init_l, first_tile, n_tiles, *, name):
    kvh, g, hd, _ = qt.shape
    tc = kc.shape[1]
    m_lanes = m0.shape[2]
    tq = m_lanes // g
    n_lat = 0 if kl is None else kl.shape[1]
    lat_specs, lat_args = [], []
    if n_lat:
        tk = kl.shape[2]
        lat_specs = [pl.BlockSpec((1, n_lat, tk, hd), lambda h, i: (h, 0, 0, 0)),
                     pl.BlockSpec((1, n_lat, hd, tk), lambda h, i: (h, 0, 0, 0))]
        lat_args = [kl, vl]
    return pl.pallas_call(
        functools.partial(_flash_kernel, init_l=init_l, n_lat=n_lat),
        grid=(kvh, n_tiles),
        in_specs=[pl.BlockSpec((1, g, hd, tq), lambda h, i: (h, 0, 0, i + first_tile)),
                  pl.BlockSpec((1, tc, hd), lambda h, i: (h, 0, 0)),
                  pl.BlockSpec((1, hd, tc), lambda h, i: (h, 0, 0)),
                  *lat_specs,
                  pl.BlockSpec((1, 1, m_lanes), lambda h, i: (h, 0, 0))],
        out_specs=pl.BlockSpec((1, g, hd, tq), lambda h, i: (h, 0, 0, i)),
        out_shape=jax.ShapeDtypeStruct((kvh, g, hd, n_tiles * tq), MXU_DTYPE),
        scratch_shapes=[pltpu.VMEM((1, m_lanes), jnp.float32),
                        pltpu.VMEM((1, m_lanes), jnp.float32),
                        pltpu.VMEM((1, m_lanes), jnp.float32),
                        pltpu.VMEM((hd, m_lanes), jnp.float32),
                        pltpu.VMEM((1, m_lanes), jnp.float32)],
        compiler_params=_params("parallel", "arbitrary"),
        name=name,
    )(qt, kc, vc, *lat_args, m0)


def _window_kernel(q_ref, kc_ref, vc_ref, kp_ref, kb_ref, kn_ref, vp_ref, vb_ref, vn_ref, bias_ref, sink_ref,
                   o_ref, *, hd):
    q = jnp.concatenate([q_ref[0, g] for g in range(q_ref.shape[1])], axis=1)
    kcat = jnp.concatenate([kc_ref[0], kp_ref[0], kb_ref[0], kn_ref[0]], axis=0)
    vcat = jnp.concatenate([vc_ref[0], vp_ref[0], vb_ref[0], vn_ref[0]], axis=1)
    s = jnp.dot(kcat, q, preferred_element_type=jnp.float32) + bias_ref[0]
    sink = sink_ref[0]
    m = jnp.maximum(sink, jnp.max(s, axis=0, keepdims=True))
    p = jnp.exp2(s - m).astype(MXU_DTYPE)
    acc = jnp.dot(vcat, p, preferred_element_type=jnp.float32)
    l = acc[hd:hd + 1] + jnp.exp2(sink - m)
    o = (acc[:hd] / l).astype(o_ref.dtype)
    for g in range(o_ref.shape[1]):
        o_ref[0, g] = o[:, g * Q_BLOCK:(g + 1) * Q_BLOCK]


def _window_bias(tc, g):
    key_off = jnp.arange(-WINDOW, 2 * Q_BLOCK)[:, None]
    q_off = jnp.arange(Q_BLOCK)[None, :]
    band = jnp.abs(q_off - key_off) <= WINDOW
    variants = [band & (key_off >= 0), band, band & (key_off < Q_BLOCK)]
    lat = jnp.stack([jnp.where(v, 0.0, NEG_INF).astype(jnp.float32) for v in variants])
    lat = jnp.tile(lat, (1, 1, g))
    return jnp.concatenate([jnp.zeros((3, tc, g * Q_BLOCK), jnp.float32), lat], axis=1)


def _window(qt, kc, vc, kl, vl, sink_m, first_tile, tc, seq):
    kvh, g, hd, _ = qt.shape
    hv = vc.shape[1]
    m_lanes = g * Q_BLOCK
    nb = seq // Q_BLOCK
    assert nb >= 2
    nkeys = tc + 3 * Q_BLOCK
    kspec = lambda f: pl.BlockSpec((1, Q_BLOCK, hd), lambda h, b: (h, f(b), 0))
    vspec = lambda f: pl.BlockSpec((1, hv, Q_BLOCK), lambda h, b: (h, 0, f(b)))
    prev = lambda b: jnp.maximum(b - 1, 0)
    nxt = lambda b: jnp.minimum(b + 1, nb - 1)
    cur = lambda b: b
    variant = lambda b: jnp.where(b == 0, 0, jnp.where(b == nb - 1, 2, 1))
    return pl.pallas_call(
        functools.partial(_window_kernel, hd=hd),
        grid=(kvh, nb),
        in_specs=[pl.BlockSpec((1, g, hd, Q_BLOCK), lambda h, b: (h, 0, 0, b + first_tile)),
                  pl.BlockSpec((1, tc, hd), lambda h, b: (h, 0, 0)),
                  pl.BlockSpec((1, hv, tc), lambda h, b: (h, 0, 0)),
                  kspec(prev), kspec(cur), kspec(nxt), vspec(prev), vspec(cur), vspec(nxt),
                  pl.BlockSpec((1, nkeys, m_lanes), lambda h, b: (variant(b), 0, 0)),
                  pl.BlockSpec((1, 1, m_lanes), lambda h, b: (h, 0, 0))],
        out_specs=pl.BlockSpec((1, g, hd, Q_BLOCK), lambda h, b: (h, 0, 0, b)),
        out_shape=jax.ShapeDtypeStruct((kvh, g, hd, seq), MXU_DTYPE),
        compiler_params=_params("parallel", "parallel"),
        name="window_attn",
    )(qt, kc, vc, kl, kl, kl, vl, vl, vl, _window_bias(tc, g), sink_m)


def _rope_angles(seq, hd):
    d_axis = hd // 2
    inv_freq = ROPE_THETA ** (-jnp.arange(0, d_axis, 2, dtype=jnp.float32) / d_axis)
    t = jnp.arange(seq)
    row = (t // GRID_W).astype(jnp.float32)[:, None] * inv_freq
    col = (t % GRID_W).astype(jnp.float32)[:, None] * inv_freq
    return jnp.concatenate([row, col], axis=1)


def _rope_tables_t(seq, tc, pad, hd):
    ang = _rope_angles(seq, hd).T
    cos = jnp.concatenate([jnp.ones((hd // 2, tc), jnp.float32), jnp.cos(ang),
                           jnp.ones((hd // 2, pad), jnp.float32)], axis=1)
    sin = jnp.concatenate([jnp.zeros((hd // 2, tc), jnp.float32), jnp.sin(ang),
                           jnp.zeros((hd // 2, pad), jnp.float32)], axis=1)
    return cos, sin


def _rope_tables(seq, tc, pad, hd):
    ang = _rope_angles(seq, hd)
    cos = jnp.concatenate([jnp.cos(ang), jnp.cos(ang)], axis=1)
    sin = jnp.concatenate([-jnp.sin(ang), jnp.sin(ang)], axis=1)
    reps = V7X_LANES // hd
    cos = jnp.tile(cos, (1, reps))
    sin = jnp.tile(sin, (1, reps))
    cos = jnp.concatenate([jnp.ones((tc, V7X_LANES), jnp.float32), cos,
                           jnp.ones((pad, V7X_LANES), jnp.float32)], axis=0)
    sin = jnp.concatenate([jnp.zeros((tc, V7X_LANES), jnp.float32), sin,
                           jnp.zeros((pad, V7X_LANES), jnp.float32)], axis=0)
    return cos, sin


def _head_perm(hd):
    f = hd // 4
    idx = np.arange(hd).reshape(2, 2, f)
    return idx.transpose(1, 0, 2).reshape(hd)


def _v_aug(vt):
    kvh, _, n = vt.shape
    ones = jnp.ones((kvh, 1, n), vt.dtype)
    zeros = jnp.zeros((kvh, V7X_BF16_ROWS - 1, n), vt.dtype)
    return jnp.concatenate([vt, ones, zeros], axis=1)


def _attention_layer(xs, h, ht, gates, w_qkv, w_o, q_norm, k_norm, sink, hd, g, tc, seq, windowed, with_ctx_out):
    t_pad, d = h.shape
    t = tc + seq
    nheads = d // hd
    kvh = nheads // g
    qd, kd = nheads * hd, kvh * hd
    perm = _head_perm(hd)
    wqt = w_qkv[:, :qd].T.reshape(nheads, hd, d)[:, perm].reshape(qd, d).astype(MXU_DTYPE)
    wk = w_qkv[:, qd:qd + kd].reshape(d, kvh, hd)[:, :, perm].reshape(d, kd).astype(MXU_DTYPE)
    wvt = w_qkv[:, qd + kd:].T.astype(MXU_DTYPE)
    qscale = (hd ** -0.5) * LOG2E
    nwq = jnp.tile(q_norm[perm] * qscale, nheads)
    nwk = jnp.tile(k_norm[perm], kvh).reshape(1, kd)
    cos, sin = _rope_tables(seq, tc, t_pad - t, hd)
    cos_t, sin_t = _rope_tables_t(seq, tc, t_pad - t, hd)
    qt = _mm_qk_t(wqt, ht, nwq, cos_t, sin_t, hd, name="proj_q").reshape(kvh, g, hd, t_pad)
    k = _mm_qk(h, wk, nwk, cos, sin, hd, name="proj_k")[:t]
    vt = _mm(wvt, ht, MXU_DTYPE, tm_prefs=(512, 256, 128), tn_prefs=TOKEN_TILES, name="proj_v")

    m_lanes = 1024
    tq = m_lanes // g
    assert tc % tq == 0 and seq % tq == 0 and (not windowed or tq == Q_BLOCK)
    kk = k.reshape(t, kvh, hd).transpose(1, 0, 2)
    vt = vt.reshape(kvh, hd, t_pad)[:, :, :t]
    if sink is None:
        m_init = jnp.full((kvh, 1, m_lanes), NEG_INF, jnp.float32)
        init_l = 0.0
    else:
        m_init = jnp.repeat(sink.reshape(kvh, g).astype(jnp.float32) * LOG2E, tq, axis=1).reshape(kvh, 1, m_lanes)
        init_l = 1.0
    kc, vc = kk[:, :tc], vt[:, :, :tc]
    if windowed:
        va = _v_aug(vt)
        o_lat = _window(qt, kc, va[:, :, :tc], kk[:, tc:], va[:, :, tc:], m_init, tc // tq, tc, seq)
    else:
        tk = _pick(seq, FLASH_KEY_CHUNKS)
        kl = kk[:, tc:].reshape(kvh, seq // tk, tk, hd)
        vl = vt[:, :, tc:].reshape(kvh, hd, seq // tk, tk).transpose(0, 2, 1, 3)
        o_lat = _flash(qt, kc, vc, kl, vl, m_init, init_l, tc // tq, seq // tq, name="global_attn")
    if with_ctx_out:
        o_ctx = _flash(qt, kc, vc, None, None, m_init, init_l, 0, tc // tq, name="ctx_attn")
    else:
        o_ctx = jnp.zeros((kvh, g, hd, tc), MXU_DTYPE)
    o_pad = jnp.zeros((kvh, g, hd, t_pad - t), MXU_DTYPE)
    ot = jnp.concatenate([o_ctx, o_lat, o_pad], axis=3).reshape(qd, t_pad)
    return _mm_res_t(w_o.T.astype(MXU_DTYPE), ot, xs, gates, tc, name="mixer_out_t")


def _dft_parts(n, scale):
    k = np.arange(n)
    ang = 2.0 * np.pi * ((k[:, None] * k[None, :]) % n) / n
    return np.cos(ang) * scale, -np.sin(ang) * scale


def _chan_dft_kernel(x_ref, c_ref, s_ref, o_ref):
    x = x_ref[...]
    o_ref[0] = jnp.dot(x, c_ref[...], preferred_element_type=jnp.float32).astype(o_ref.dtype)
    o_ref[1] = jnp.dot(x, s_ref[...], preferred_element_type=jnp.float32).astype(o_ref.dtype)


def _chan_dft(h, gc):
    t, d = h.shape
    re, im = _dft_parts(gc, gc ** -0.5)
    tm = _pick(t, ROW_TILES)
    return pl.pallas_call(
        _chan_dft_kernel,
        grid=(t // tm, d // gc),
        in_specs=[pl.BlockSpec((tm, gc), lambda i, j: (i, j)),
                  pl.BlockSpec((gc, gc), lambda i, j: (0, 0)),
                  pl.BlockSpec((gc, gc), lambda i, j: (0, 0))],
        out_specs=pl.BlockSpec((2, tm, gc), lambda i, j: (0, i, j)),
        out_shape=jax.ShapeDtypeStruct((2, t, d), MXU_DTYPE),
        compiler_params=_params("parallel", "parallel"),
        name="chan_dft",
    )(h, jnp.asarray(re, MXU_DTYPE), jnp.asarray(im, MXU_DTYPE))


def _pos_stage1_kernel(m_ref, y_ref, tr_ref, ti_ref, o_ref, *, n1):
    y = jnp.concatenate([y_ref[0], y_ref[1]], axis=0)
    a = jnp.dot(m_ref[...], y, preferred_element_type=jnp.float32)
    ar, ai = a[:n1], a[n1:]
    tr, ti = tr_ref[0][:, 0:1], ti_ref[0][:, 0:1]
    o_ref[0] = (ar * tr - ai * ti).astype(o_ref.dtype)
    o_ref[1] = (ar * ti + ai * tr).astype(o_ref.dtype)


def _pos_stage2_kernel(m_ref, b_ref, o_ref):
    b = jnp.concatenate([b_ref[0, 0], b_ref[1, 0]], axis=0)
    o_ref[0] = jnp.dot(m_ref[...], b, preferred_element_type=jnp.float32).astype(o_ref.dtype)


def _pos_dft_two_stage(y, n1, n2):
    _, s, d = y.shape
    sc = float(s) ** -0.25
    r1, i1 = _dft_parts(n1, sc)
    m1 = np.block([[r1, -i1], [i1, r1]])
    f1 = np.arange(n1)[None, :]
    t2 = np.arange(n2)[:, None]
    ang = 2.0 * np.pi * (t2 * f1) / s
    tw_r = np.broadcast_to(np.cos(ang)[:, :, None], (n2, n1, V7X_LANES))
    tw_i = np.broadcast_to(-np.sin(ang)[:, :, None], (n2, n1, V7X_LANES))
    y4 = y.reshape(2, n1, n2 * d)
    b = pl.pallas_call(
        functools.partial(_pos_stage1_kernel, n1=n1),
        grid=(n2,),
        in_specs=[pl.BlockSpec((2 * n1, 2 * n1), lambda j: (0, 0)),
                  pl.BlockSpec((2, n1, d), lambda j: (0, 0, j)),
                  pl.BlockSpec((1, n1, V7X_LANES), lambda j: (j, 0, 0)),
                  pl.BlockSpec((1, n1, V7X_LANES), lambda j: (j, 0, 0))],
        out_specs=pl.BlockSpec((2, n1, d), lambda j: (0, 0, j)),
        out_shape=jax.ShapeDtypeStruct((2, n1, n2 * d), MXU_DTYPE),
        compiler_params=_params("parallel"),
        name="pos_dft_stage1",
    )(jnp.asarray(m1, MXU_DTYPE), y4, jnp.asarray(tw_r, jnp.float32), jnp.asarray(tw_i, jnp.float32))
    b4 = b.reshape(2, n1, n2, d)
    r2, i2 = _dft_parts(n2, sc)
    m2 = np.concatenate([r2, -i2], axis=1)
    tcn = _pick(d, (2048, 1024, 512, 256, 128))
    out = pl.pallas_call(
        _pos_stage2_kernel,
        grid=(n1, d // tcn),
        in_specs=[pl.BlockSpec((n2, 2 * n2), lambda f, j: (0, 0)),
                  pl.BlockSpec((2, 1, n2, tcn), lambda f, j: (0, f, 0, j))],
        out_specs=pl.BlockSpec((1, n2, tcn), lambda f, j: (f, 0, j)),
        out_shape=jax.ShapeDtypeStruct((n1, n2, d), MXU_DTYPE),
        compiler_params=_params("parallel", "parallel"),
        name="pos_dft_stage2",
    )(jnp.asarray(m2, MXU_DTYPE), b4)
    return out.transpose(1, 0, 2).reshape(s, d)


def _fourier_mix(h, tc, seq):
    t_pad, d = h.shape
    y = _chan_dft(h, d // FNET_GROUPS)
    rc, ic = _dft_parts(tc, tc ** -0.5)
    mc = np.concatenate([rc, -ic], axis=1)
    yc = jnp.concatenate([y[0, :tc], y[1, :tc]], axis=0)
    mixed_c = _mm(jnp.asarray(mc, MXU_DTYPE), yc, MXU_DTYPE, name="ctx_pos_dft")
    n1 = 1 << (int(math.log2(seq)) // 2)
    n2 = seq // n1
    mixed_l = _pos_dft_two_stage(y[:, tc:tc + seq], n1, n2)
    return jnp.concatenate([mixed_c, mixed_l, jnp.zeros((t_pad - tc - seq, d), MXU_DTYPE)], axis=0)


def _top16(s):
    n = s.shape[0]
    rows = lax.broadcasted_iota(jnp.int32, s.shape, 0)
    row16 = lax.broadcasted_iota(jnp.int32, (PEER_TOPK, s.shape[1]), 0)
    rank = jnp.full(s.shape, RANK_NONE, jnp.float32)
    sv = jnp.zeros((PEER_TOPK, s.shape[1]), jnp.float32)
    v = s
    for k in range(PEER_TOPK):
        m = jnp.max(v, axis=0, keepdims=True)
        idx = jnp.min(jnp.where(v == m, rows, n), axis=0, keepdims=True)
        hit = rows == idx
        rank = jnp.where(hit, float(k), rank)
        v = jnp.where(hit, -jnp.inf, v)
        sv = jnp.where(row16 == k, m, sv)
    return rank, sv


_CAND_GROUPS = ((0, 0, 8), (0, 8, 8), (1, 0, 8), (2, 0, 5), (3, 0, 4), (4, 0, 3), (5, 0, 2), (6, 0, 2), (7, 0, 2))


def _select_pairs(sv0, sv1):
    lanes = sv0.shape[1]
    sub = lax.broadcasted_iota(jnp.int32, (V7X_SUBLANES, lanes), 0)
    cands, poss = [], []
    for r0, r1_0, cnt in _CAND_GROUPS:
        c = sv0[r0:r0 + 1] + sv1[r1_0:r1_0 + V7X_SUBLANES]
        cands.append(jnp.where(sub < cnt, c, -jnp.inf))
        poss.append(r0 * PEER_TOPK + r1_0 + sub)
    cands.append(sv0[V7X_SUBLANES:] + sv1[0:1])
    poss.append((V7X_SUBLANES + sub) * PEER_TOPK)
    cand = jnp.concatenate(cands, axis=0)
    pos = jnp.concatenate(poss, axis=0)
    cmax = sv0[0:1] + sv1[0:1]
    big = PEER_TOPK * PEER_TOPK
    c = cand
    sel = jnp.zeros(cand.shape, jnp.float32)
    for _ in range(PEER_TOPK):
        m = jnp.max(c, axis=0, keepdims=True)
        pidx = jnp.min(jnp.where(c == m, pos, big), axis=0, keepdims=True)
        hit = pos == pidx
        sel = jnp.where(hit, 1.0, sel)
        c = jnp.where(hit, -jnp.inf, c)
    z = jnp.sum(jnp.where(sel > 0.0, jnp.exp(cand - cmax), 0.0), axis=0, keepdims=True)
    g = V7X_SUBLANES
    rows = [jnp.sum(sel[0:2 * g], axis=0, keepdims=True)]
    for i in range(2, len(_CAND_GROUPS)):
        rows.append(jnp.sum(sel[i * g:(i + 1) * g], axis=0, keepdims=True))
    rows.append(sel[len(_CAND_GROUPS) * g:])
    return jnp.concatenate(rows, axis=0), z


def _peer_select_kernel(q_ref, keys_ref, r1_ref, b_ref, cnt_ref, a_ref):
    def head(hh, carry):
        base = pl.multiple_of(hh * 2 * PEER_HALF, 2 * PEER_HALF)
        q0 = q_ref[pl.ds(base, PEER_HALF), :]
        q1 = q_ref[pl.ds(base + PEER_HALF, PEER_HALF), :]
        s0 = jnp.dot(keys_ref[hh, 0], q0, preferred_element_type=jnp.float32, precision=lax.Precision.HIGHEST)
        s1 = jnp.dot(keys_ref[hh, 1], q1, preferred_element_type=jnp.float32, precision=lax.Precision.HIGHEST)
        rank0, sv0 = _top16(s0)
        rank1, sv1 = _top16(s1)
        cnt16, z = _select_pairs(sv0, sv1)
        cnt = jnp.zeros(s0.shape, jnp.float32)
        for r in range(PEER_TOPK):
            cnt = jnp.where(rank0 == float(r), cnt16[r:r + 1], cnt)
        r1_ref[hh] = rank1
        b_ref[hh] = jnp.exp(s1 - sv1[0:1])
        cnt_ref[hh] = cnt
        a_ref[hh] = jnp.exp(s0 - sv0[0:1]) / z
        return carry

    lax.fori_loop(0, PEER_HEADS, head, 0, unroll=4)


def _peer_select(qt, sub_keys):
    _, t = qt.shape
    tl = _pick(t, (2 * V7X_LANES, V7X_LANES))
    shp = jax.ShapeDtypeStruct((PEER_HEADS, PEER_N_KEYS, t), jnp.float32)
    ospec = pl.BlockSpec((PEER_HEADS, PEER_N_KEYS, tl), lambda i: (0, 0, i))
    return pl.pallas_call(
        _peer_select_kernel,
        grid=(t // tl,),
        in_specs=[pl.BlockSpec((qt.shape[0], tl), lambda i: (0, i)),
                  pl.BlockSpec(sub_keys.shape, lambda i: (0, 0, 0, 0))],
        out_specs=[ospec] * 4,
        out_shape=[shp] * 4,
        compiler_params=_params("parallel"),
        name="peer_select",
    )(qt, sub_keys)


def _gelu_exact(x):
    return 0.5 * x * (1.0 + lax.erf(x * (2.0 ** -0.5)))


def _peer_act_kernel(u_ref, h_ref, r1_ref, b_ref, cnt_ref, a_ref, o_ref, *, iblocks):
    nk = PEER_N_KEYS
    groups = [slice(g * MXU_COLS, (g + 1) * MXU_COLS) for g in range(h_ref.shape[1] // MXU_COLS)]
    pres = [jnp.dot(u_ref[...], h_ref[:, cols], preferred_element_type=jnp.float32) for cols in groups]
    dt = o_ref.dtype
    for pre, cols in zip(pres, groups):
        r1 = [r1_ref[hh, :, cols].astype(dt) for hh in range(PEER_HEADS)]
        b = [b_ref[hh, :, cols].astype(dt) for hh in range(PEER_HEADS)]
        for ii in range(iblocks):
            w = jnp.zeros((nk, MXU_COLS), dt)
            for hh in range(PEER_HEADS):
                sel = r1[hh] < cnt_ref[hh, ii:ii + 1, cols].astype(dt)
                w = w + jnp.where(sel, b[hh], jnp.zeros_like(b[hh])) * a_ref[hh, ii:ii + 1, cols].astype(dt)
            o_ref[ii * nk:(ii + 1) * nk, cols] = _gelu_exact(pre[ii * nk:(ii + 1) * nk]).astype(dt) * w


def _peer_act(u, layer, ht, r1, b, cnt, a):
    _, e, d = u.shape
    _, t = ht.shape
    iblocks = V7X_SUBLANES
    te = iblocks * PEER_N_KEYS
    tt = _pick(t, TOKEN_TILES)
    full = pl.BlockSpec((PEER_HEADS, PEER_N_KEYS, tt), lambda i, j: (0, 0, i))
    part = pl.BlockSpec((PEER_HEADS, iblocks, tt), lambda i, j: (0, j, i))
    return pl.pallas_call(
        functools.partial(_peer_act_kernel, iblocks=iblocks),
        grid=(t // tt, e // te),
        in_specs=[pl.BlockSpec((None, te, d), lambda i, j: (layer, j, 0)),
                  pl.BlockSpec((d, tt), lambda i, j: (0, i)),
                  full, full, part, part],
        out_specs=pl.BlockSpec((te, tt), lambda i, j: (j, i)),
        out_shape=jax.ShapeDtypeStruct((e, t), MXU_DTYPE),
        compiler_params=_params("parallel", "parallel"),
        name="peer_act",
    )(u, ht, r1, b, cnt, a)


def _peer(x, nw, mod_ctx, mod_lat, gates, tc, w_q, sub_keys, u_all, vt_all, layer):
    (ht,) = _modulate(x, nw, mod_ctx, mod_lat, tc, ("feature_major",))
    qt = _mm(w_q.T.astype(MXU_DTYPE), ht, jnp.float32, tm_prefs=(512, 256, 128),
             tn_prefs=TOKEN_TILES, name="peer_query")
    r1, b, cnt, a = _peer_select(qt, sub_keys)
    act = _peer_act(u_all, layer, ht, r1, b, cnt, a)
    return _mm_res_t(vt_all, act, x, gates, tc, layer=layer, name="peer_out")


def kernel(x, c, ctx, c_ctx, ada_w, ada_b, mix_norm, ffn_norm, a_w_qkv, a_w_o, a_q_norm, a_k_norm,
           b_w_qkv, b_w_o, b_q_norm, b_k_norm, b_sink, f_w_out, peer_w_q, peer_sub_keys, peer_u, peer_v):
    batch, seq, d = x.shape
    assert batch == 1 and ctx.shape[0] == 1
    tc = ctx.shape[1]
    depth = ada_w.shape[0]
    pad = -(tc + seq) % TOKEN_TILES[0]
    xs = jnp.concatenate([ctx[0], x[0], jnp.zeros((pad, d), x.dtype)], axis=0)
    mods = _ada(jnp.stack([c[0], c_ctx]), ada_w, ada_b)
    mods = mods.reshape(depth, 2, N_MOD, d)
    u_all = peer_u.astype(MXU_DTYPE)
    vt_all = peer_v.transpose(0, 2, 1).astype(MXU_DTYPE)
    for layer in range(depth):
        last = layer == depth - 1
        kind, j = layer % N_MIXERS, layer // N_MIXERS
        lat, cx = mods[layer, 0], mods[layer, 1]
        gates = jnp.stack([cx[2], lat[2]])
        if kind == 2:
            (h,) = _modulate(xs, mix_norm[layer], cx[0:2], lat[0:2], tc, ("token_major",))
            xs = _mm_res(_fourier_mix(h, tc, seq), f_w_out[j].astype(MXU_DTYPE), xs, gates, tc, name="mixer_out")
        else:
            h, ht = _modulate(xs, mix_norm[layer], cx[0:2], lat[0:2], tc, ("token_major", "feature_major"))
            if kind == 0:
                xs = _attention_layer(xs, h, ht, gates, a_w_qkv[j], a_w_o[j], a_q_norm[j], a_k_norm[j], None,
                                      A_HEAD_DIM, A_GROUP, tc, seq, False, not last)
            else:
                xs = _attention_layer(xs, h, ht, gates, b_w_qkv[j], b_w_o[j], b_q_norm[j], b_k_norm[j], b_sink[j],
                                      B_HEAD_DIM, B_GROUP, tc, seq, True, not last)
        xs = _peer(xs, ffn_norm[layer], cx[3:5], lat[3:5], jnp.stack([cx[5], lat[5]]), tc,
                   peer_w_q[layer], peer_sub_keys[layer], u_all, vt_all, layer)
    return xs[tc:tc + seq][None]
```

```python
import functools
import math

import numpy as np
import jax
import jax.numpy as jnp
from jax import lax
from jax.experimental import pallas as pl
from jax.experimental.pallas import tpu as pltpu

GRID_W = 64
Q_BLOCK = 128
ROPE_THETA = 10000.0
NORM_EPS = 1e-6
NEG_INF = -1e30
N_MOD = 6
N_MIXERS = 3
A_HEAD_DIM = 128
A_GROUP = 4
B_HEAD_DIM = 64
B_GROUP = 8
WINDOW = 128
FNET_GROUPS = 8
PEER_HEADS = 8
PEER_N_KEYS = 96
PEER_TOPK = 16
PEER_HALF = 128

V7X_LANES = 128
V7X_SUBLANES = 8
V7X_BF16_ROWS = 16
MXU_COLS = 256
V7X_VMEM_BYTES = 64 * 1024 * 1024
VMEM_LIMIT = V7X_VMEM_BYTES - 8 * 1024 * 1024

MXU_DTYPE = jnp.bfloat16
LOG2E = 1.4426950408889634
RANK_NONE = 99.0
ROW_TILES = (1536, 768, 512, 256, 128)
TOKEN_TILES = (768, 512, 256)
ATTN_LANES = 1024
FLASH_KEY_CHUNKS = (4096, 2048, 1024, 512, 256)
EXP2_HEADROOM = 64.0


def _params(*sem):
    return pltpu.CompilerParams(dimension_semantics=sem, vmem_limit_bytes=VMEM_LIMIT)


def _pick(n, prefs):
    for p in prefs:
        if n % p == 0:
            return p
    return n


def _ada_kernel(cond_ref, w_ref, b_ref, o_ref):
    w = w_ref[0]
    reps = w.shape[1] // V7X_LANES
    rows = []
    for r in range(2):
        c = cond_ref[r]
        s = c * jax.nn.sigmoid(c)
        rows.append(jnp.sum(w * jnp.tile(s, (1, reps)), axis=0, keepdims=True))
    o_ref[0] = jnp.concatenate(rows, axis=0) + b_ref[0]


def _ada(cond2, ada_w, ada_b):
    depth, d, n = ada_w.shape
    tn = _pick(n, (512, 256, 128))
    cond_rep = jnp.broadcast_to(cond2[:, :, None], (2, d, V7X_LANES))
    return pl.pallas_call(
        _ada_kernel,
        grid=(depth, n // tn),
        in_specs=[pl.BlockSpec((2, d, V7X_LANES), lambda l, j: (0, 0, 0)),
                  pl.BlockSpec((1, d, tn), lambda l, j: (l, 0, j)),
                  pl.BlockSpec((1, 1, tn), lambda l, j: (l, 0, j))],
        out_specs=pl.BlockSpec((1, 2, tn), lambda l, j: (l, 0, j)),
        out_shape=jax.ShapeDtypeStruct((depth, 2, n), jnp.float32),
        compiler_params=_params("parallel", "parallel"),
        name="ada",
    )(cond_rep, ada_w, ada_b.reshape(depth, 1, n))


def _modulate_kernel(x_ref, nw_ref, mod_ref, *o_refs, layouts):
    x = x_ref[...]
    y = x * lax.rsqrt(jnp.mean(x * x, axis=-1, keepdims=True) + NORM_EPS) * nw_ref[...]
    shift = mod_ref[0, 0:1, :]
    scale = mod_ref[0, 1:2, :]
    h = y * (1.0 + scale) + shift
    for o_ref, layout in zip(o_refs, layouts):
        o_ref[...] = (h.T if layout == "feature_major" else h).astype(o_ref.dtype)


def _modulate(x, nw, mod_ctx, mod_lat, tc, layouts):
    t, d = x.shape
    tm = _pick(math.gcd(t, tc), (256, 128))
    ctx_tiles = tc // tm
    mods = jnp.stack([mod_ctx, mod_lat])
    shapes = [jax.ShapeDtypeStruct((d, t) if lay == "feature_major" else (t, d), MXU_DTYPE) for lay in layouts]
    specs = [pl.BlockSpec((d, tm), lambda i: (0, i)) if lay == "feature_major"
             else pl.BlockSpec((tm, d), lambda i: (i, 0)) for lay in layouts]
    return pl.pallas_call(
        functools.partial(_modulate_kernel, layouts=tuple(layouts)),
        grid=(t // tm,),
        in_specs=[pl.BlockSpec((tm, d), lambda i: (i, 0)),
                  pl.BlockSpec((1, d), lambda i: (0, 0)),
                  pl.BlockSpec((1, 2, d), lambda i: (jnp.where(i < ctx_tiles, 0, 1), 0, 0))],
        out_specs=specs,
        out_shape=shapes,
        compiler_params=_params("parallel"),
        name="modulate",
    )(x, nw.reshape(1, d), mods)


def _mm_kernel(a_ref, b_ref, o_ref):
    o_ref[...] = jnp.dot(a_ref[...], b_ref[...], preferred_element_type=jnp.float32).astype(o_ref.dtype)


def _mm(a, b, out_dtype, *, tm_prefs=ROW_TILES, tn_prefs=(512, 256, 128), name="mm"):
    m, k = a.shape
    _, n = b.shape
    tm, tn = _pick(m, tm_prefs), _pick(n, tn_prefs)
    return pl.pallas_call(
        _mm_kernel,
        grid=(m // tm, n // tn),
        in_specs=[pl.BlockSpec((tm, k), lambda i, j: (i, 0)),
                  pl.BlockSpec((k, tn), lambda i, j: (0, j))],
        out_specs=pl.BlockSpec((tm, tn), lambda i, j: (i, j)),
        out_shape=jax.ShapeDtypeStruct((m, n), out_dtype),
        compiler_params=_params("parallel", "parallel"),
        name=name,
    )(a, b)


def _mm_res_kernel(a_ref, b_ref, res_ref, g_ref, o_ref, *, tc, transposed):
    acc = jnp.dot(a_ref[...], b_ref[...], preferred_element_type=jnp.float32)
    if transposed:
        acc = acc.T
        row0 = pl.program_id(0) * acc.shape[0]
    else:
        row0 = pl.program_id(0) * acc.shape[0]
    rid = row0 + lax.broadcasted_iota(jnp.int32, acc.shape, 0)
    gate = jnp.where(rid < tc, g_ref[0:1, :], g_ref[1:2, :])
    o_ref[...] = res_ref[...] + gate * acc


def _mm_res(a, b, res, gates, tc, *, name="mm_res"):
    m, k = a.shape
    _, n = b.shape
    tm, tn = _pick(m, ROW_TILES), _pick(n, (512, 256, 128))
    return pl.pallas_call(
        functools.partial(_mm_res_kernel, tc=tc, transposed=False),
        grid=(m // tm, n // tn),
        in_specs=[pl.BlockSpec((tm, k), lambda i, j: (i, 0)),
                  pl.BlockSpec((k, tn), lambda i, j: (0, j)),
                  pl.BlockSpec((tm, tn), lambda i, j: (i, j)),
                  pl.BlockSpec((2, tn), lambda i, j: (0, j))],
        out_specs=pl.BlockSpec((tm, tn), lambda i, j: (i, j)),
        out_shape=jax.ShapeDtypeStruct((m, n), jnp.float32),
        compiler_params=_params("parallel", "parallel"),
        name=name,
    )(a, b, res, gates)


def _mm_res_t(vt, at, res, gates, tc, *, layer=None, name="mm_res_t"):
    n, k = vt.shape[-2:]
    _, t = at.shape
    vspec = (lambda tn: pl.BlockSpec((tn, k), lambda i, j: (j, 0))) if layer is None else (
        lambda tn: pl.BlockSpec((None, tn, k), lambda i, j: (layer, j, 0)))
    tt, tn = _pick(t, TOKEN_TILES), _pick(n, (512, 256, 128) if k <= 4096 else (256, 128))
    return pl.pallas_call(
        functools.partial(_mm_res_kernel, tc=tc, transposed=True),
        grid=(t // tt, n // tn),
        in_specs=[vspec(tn),
                  pl.BlockSpec((k, tt), lambda i, j: (0, i)),
                  pl.BlockSpec((tt, tn), lambda i, j: (i, j)),
                  pl.BlockSpec((2, tn), lambda i, j: (0, j))],
        out_specs=pl.BlockSpec((tt, tn), lambda i, j: (i, j)),
        out_shape=jax.ShapeDtypeStruct((t, n), jnp.float32),
        compiler_params=_params("parallel", "parallel"),
        name=name,
    )(vt, at, res, gates)


def _mm_qk_t_kernel(w_ref, h_ref, nw_ref, cos_ref, sin_ref, o_ref, *, hd):
    acc = jnp.dot(w_ref[...], h_ref[...], preferred_element_type=jnp.float32)
    cos = cos_ref[...]
    sin = sin_ref[...]
    reps = acc.shape[1] // V7X_LANES
    half = hd // 2
    for r in range(acc.shape[0] // hd):
        x = acc[r * hd:(r + 1) * hd]
        ms = jnp.sum(x * x, axis=0, keepdims=True) * (1.0 / hd)
        y = x * lax.rsqrt(ms + NORM_EPS) * jnp.tile(nw_ref[r * hd:(r + 1) * hd], (1, reps))
        y1, y2 = y[:half], y[half:]
        o_ref[r * hd:r * hd + half] = (y1 * cos - y2 * sin).astype(o_ref.dtype)
        o_ref[r * hd + half:(r + 1) * hd] = (y1 * sin + y2 * cos).astype(o_ref.dtype)


def _mm_qk_t(wt, ht, nw, cos, sin, hd, *, name):
    n, k = wt.shape
    _, t = ht.shape
    tf, tt = _pick(n, (512, 256, 128)), _pick(t, TOKEN_TILES)
    nw_rep = jnp.broadcast_to(nw[:, None], (n, V7X_LANES))
    return pl.pallas_call(
        functools.partial(_mm_qk_t_kernel, hd=hd),
        grid=(t // tt, n // tf),
        in_specs=[pl.BlockSpec((tf, k), lambda i, j: (j, 0)),
                  pl.BlockSpec((k, tt), lambda i, j: (0, i)),
                  pl.BlockSpec((tf, V7X_LANES), lambda i, j: (j, 0)),
                  pl.BlockSpec((hd // 2, tt), lambda i, j: (0, i)),
                  pl.BlockSpec((hd // 2, tt), lambda i, j: (0, i))],
        out_specs=pl.BlockSpec((tf, tt), lambda i, j: (j, i)),
        out_shape=jax.ShapeDtypeStruct((n, t), MXU_DTYPE),
        compiler_params=_params("parallel", "parallel"),
        name=name,
    )(wt, ht, nw_rep, cos, sin)


def _flash_kernel(q_ref, kc_ref, vc_ref, *rest, init_l, n_lat):
    if n_lat:
        kl_ref, vl_ref, m0_ref, o_ref, mrun_ref, macc_ref, l_ref, acc_ref, knorm_ref = rest
    else:
        m0_ref, o_ref, mrun_ref, macc_ref, l_ref, acc_ref, knorm_ref = rest
    q = jnp.concatenate([q_ref[0, g] for g in range(q_ref.shape[1])], axis=1)

    def norm2(k):
        kf = k.astype(jnp.float32)
        return jnp.max(jnp.sum(kf * kf, axis=0, keepdims=True), axis=1, keepdims=True)

    if n_lat:
        @pl.when(pl.program_id(1) == 0)
        def _():
            k2 = lax.fori_loop(0, n_lat, lambda c, mx: jnp.maximum(mx, norm2(kl_ref[0, c])), norm2(kc_ref[0]))
            knorm_ref[...] = jnp.broadcast_to(jnp.sqrt(k2), knorm_ref.shape)

    mrun_ref[...] = m0_ref[0]
    macc_ref[...] = m0_ref[0]
    l_ref[...] = jnp.full(l_ref.shape, init_l, jnp.float32)
    acc_ref[...] = jnp.zeros(acc_ref.shape, jnp.float32)

    def accumulate(p, v, m_to):
        alpha = jnp.exp2(macc_ref[...] - m_to)
        l_ref[...] = l_ref[...] * alpha + jnp.sum(p, axis=0, keepdims=True)
        acc_ref[...] = acc_ref[...] * alpha + jnp.dot(v, p.astype(MXU_DTYPE), preferred_element_type=jnp.float32)
        macc_ref[...] = m_to

    def scores(k):
        return lax.dot_general(k, q, (((0,), (0,)), ((), ())), preferred_element_type=jnp.float32)

    def own_max(k, v):
        s = scores(k)
        m_new = jnp.maximum(mrun_ref[...], jnp.max(s, axis=0, keepdims=True))
        accumulate(jnp.exp2(s - m_new), v, m_new)
        mrun_ref[...] = m_new

    def earlier_max(k, v):
        m_use = mrun_ref[...]
        s = scores(k)
        accumulate(jnp.exp2(s - m_use), v, m_use)
        mrun_ref[...] = jnp.maximum(m_use, jnp.max(s, axis=0, keepdims=True))

    own_max(kc_ref[0], vc_ref[0])
    if n_lat:
        qf = q.astype(jnp.float32)
        qnorm = jnp.sqrt(jnp.sum(qf * qf, axis=0, keepdims=True))
        bounded = jnp.max(qnorm * knorm_ref[...] - mrun_ref[...]) <= EXP2_HEADROOM

        def loop(step):
            def body(c, carry):
                step(kl_ref[0, c], vl_ref[0, c])
                return carry
            lax.fori_loop(0, n_lat, body, 0)

        pl.when(bounded)(lambda: loop(earlier_max))
        pl.when(jnp.logical_not(bounded))(lambda: loop(own_max))

    o = (acc_ref[...] / l_ref[...]).astype(o_ref.dtype)
    tq = o_ref.shape[3]
    for g in range(o_ref.shape[1]):
        o_ref[0, g] = o[:, g * tq:(g + 1) * tq]


def _flash(qt, kc, vc, kl, vl, m0, init_l, first_tile, n_tiles, *, name):
    kvh, g, hd, _ = qt.shape
    tc = kc.shape[2]
    m_lanes = m0.shape[2]
    tq = m_lanes // g
    n_lat = 0 if kl is None else kl.shape[1]
    lat_specs, lat_args = [], []
    if n_lat:
        tk = kl.shape[3]
        lat_specs = [pl.BlockSpec((1, n_lat, hd, tk), lambda h, i: (h, 0, 0, 0))] * 2
        lat_args = [kl, vl]
    return pl.pallas_call(
        functools.partial(_flash_kernel, init_l=init_l, n_lat=n_lat),
        grid=(kvh, n_tiles),
        in_specs=[pl.BlockSpec((1, g, hd, tq), lambda h, i: (h, 0, 0, i + first_tile)),
                  pl.BlockSpec((1, hd, tc), lambda h, i: (h, 0, 0)),
                  pl.BlockSpec((1, hd, tc), lambda h, i: (h, 0, 0)),
                  *lat_specs,
                  pl.BlockSpec((1, 1, m_lanes), lambda h, i: (h, 0, 0))],
        out_specs=pl.BlockSpec((1, g, hd, tq), lambda h, i: (h, 0, 0, i)),
        out_shape=jax.ShapeDtypeStruct((kvh, g, hd, n_tiles * tq), MXU_DTYPE),
        scratch_shapes=[pltpu.VMEM((1, m_lanes), jnp.float32),
                        pltpu.VMEM((1, m_lanes), jnp.float32),
                        pltpu.VMEM((1, m_lanes), jnp.float32),
                        pltpu.VMEM((hd, m_lanes), jnp.float32),
                        pltpu.VMEM((1, m_lanes), jnp.float32)],
        compiler_params=_params("parallel", "arbitrary"),
        name=name,
    )(qt, kc, vc, *lat_args, m0)


def _window_kernel(q_ref, kc_ref, vc_ref, kp_ref, kb_ref, kn_ref, vp_ref, vb_ref, vn_ref, bias_ref, sink_ref,
                   o_ref, *, hd):
    q = jnp.concatenate([q_ref[0, g] for g in range(q_ref.shape[1])], axis=1)
    kcat = jnp.concatenate([kc_ref[0], kp_ref[0], kb_ref[0], kn_ref[0]], axis=1)
    vcat = jnp.concatenate([vc_ref[0], vp_ref[0], vb_ref[0], vn_ref[0]], axis=1)
    s = lax.dot_general(kcat, q, (((0,), (0,)), ((), ())), preferred_element_type=jnp.float32) + bias_ref[0]
    sink = sink_ref[0]
    m = jnp.maximum(sink, jnp.max(s, axis=0, keepdims=True))
    p = jnp.exp2(s - m).astype(MXU_DTYPE)
    acc = jnp.dot(vcat, p, preferred_element_type=jnp.float32)
    l = acc[hd:hd + 1] + jnp.exp2(sink - m)
    o = (acc[:hd] / l).astype(o_ref.dtype)
    for g in range(o_ref.shape[1]):
        o_ref[0, g] = o[:, g * Q_BLOCK:(g + 1) * Q_BLOCK]


def _window_bias(tc, g):
    key_off = jnp.arange(-WINDOW, 2 * Q_BLOCK)[:, None]
    q_off = jnp.arange(Q_BLOCK)[None, :]
    band = jnp.abs(q_off - key_off) <= WINDOW
    variants = [band & (key_off >= 0), band, band & (key_off < Q_BLOCK)]
    lat = jnp.stack([jnp.where(v, 0.0, NEG_INF).astype(jnp.float32) for v in variants])
    lat = jnp.tile(lat, (1, 1, g))
    return jnp.concatenate([jnp.zeros((3, tc, g * Q_BLOCK), jnp.float32), lat], axis=1)


def _window(qt, kc, vc, kl, vl, sink_m, first_tile, tc, seq):
    kvh, g, hd, _ = qt.shape
    hv = vc.shape[1]
    m_lanes = g * Q_BLOCK
    nb = seq // Q_BLOCK
    assert nb >= 2
    nkeys = tc + 3 * Q_BLOCK
    kspec = lambda f: pl.BlockSpec((1, hd, Q_BLOCK), lambda h, b: (h, 0, f(b)))
    vspec = lambda f: pl.BlockSpec((1, hv, Q_BLOCK), lambda h, b: (h, 0, f(b)))
    prev = lambda b: jnp.maximum(b - 1, 0)
    nxt = lambda b: jnp.minimum(b + 1, nb - 1)
    cur = lambda b: b
    variant = lambda b: jnp.where(b == 0, 0, jnp.where(b == nb - 1, 2, 1))
    return pl.pallas_call(
        functools.partial(_window_kernel, hd=hd),
        grid=(kvh, nb),
        in_specs=[pl.BlockSpec((1, g, hd, Q_BLOCK), lambda h, b: (h, 0, 0, b + first_tile)),
                  pl.BlockSpec((1, hd, tc), lambda h, b: (h, 0, 0)),
                  pl.BlockSpec((1, hv, tc), lambda h, b: (h, 0, 0)),
                  kspec(prev), kspec(cur), kspec(nxt), vspec(prev), vspec(cur), vspec(nxt),
                  pl.BlockSpec((1, nkeys, m_lanes), lambda h, b: (variant(b), 0, 0)),
                  pl.BlockSpec((1, 1, m_lanes), lambda h, b: (h, 0, 0))],
        out_specs=pl.BlockSpec((1, g, hd, Q_BLOCK), lambda h, b: (h, 0, 0, b)),
        out_shape=jax.ShapeDtypeStruct((kvh, g, hd, seq), MXU_DTYPE),
        compiler_params=_params("parallel", "parallel"),
        name="window_attn",
    )(qt, kc, vc, kl, kl, kl, vl, vl, vl, _window_bias(tc, g), sink_m)


def _rope_angles(seq, hd):
    d_axis = hd // 2
    inv_freq = ROPE_THETA ** (-jnp.arange(0, d_axis, 2, dtype=jnp.float32) / d_axis)
    t = jnp.arange(seq)
    row = (t // GRID_W).astype(jnp.float32)[:, None] * inv_freq
    col = (t % GRID_W).astype(jnp.float32)[:, None] * inv_freq
    return jnp.concatenate([row, col], axis=1)


def _rope_tables_t(seq, tc, pad, hd):
    ang = _rope_angles(seq, hd).T
    cos = jnp.concatenate([jnp.ones((hd // 2, tc), jnp.float32), jnp.cos(ang),
                           jnp.ones((hd // 2, pad), jnp.float32)], axis=1)
    sin = jnp.concatenate([jnp.zeros((hd // 2, tc), jnp.float32), jnp.sin(ang),
                           jnp.zeros((hd // 2, pad), jnp.float32)], axis=1)
    return cos, sin


def _head_perm(hd):
    f = hd // 4
    idx = np.arange(hd).reshape(2, 2, f)
    return idx.transpose(1, 0, 2).reshape(hd)


def _v_aug(vt):
    kvh, _, n = vt.shape
    ones = jnp.ones((kvh, 1, n), vt.dtype)
    zeros = jnp.zeros((kvh, V7X_BF16_ROWS - 1, n), vt.dtype)
    return jnp.concatenate([vt, ones, zeros], axis=1)


def _attention_layer(xs, ht, gates, w_qkv, w_o, q_norm, k_norm, sink, hd, g, tc, seq, windowed, with_ctx_out):
    d, t_pad = ht.shape
    t = tc + seq
    nheads = d // hd
    kvh = nheads // g
    qd, kd = nheads * hd, kvh * hd
    perm = _head_perm(hd)
    wqt = w_qkv[:, :qd].T.reshape(nheads, hd, d)[:, perm].reshape(qd, d).astype(MXU_DTYPE)
    wkt = w_qkv[:, qd:qd + kd].T.reshape(kvh, hd, d)[:, perm].reshape(kd, d).astype(MXU_DTYPE)
    wvt = w_qkv[:, qd + kd:].T.astype(MXU_DTYPE)
    qscale = (hd ** -0.5) * LOG2E
    nwq = jnp.tile(q_norm[perm] * qscale, nheads)
    nwk = jnp.tile(k_norm[perm], kvh)
    cos_t, sin_t = _rope_tables_t(seq, tc, t_pad - t, hd)
    qt = _mm_qk_t(wqt, ht, nwq, cos_t, sin_t, hd, name="proj_q").reshape(kvh, g, hd, t_pad)
    kk = _mm_qk_t(wkt, ht, nwk, cos_t, sin_t, hd, name="proj_k").reshape(kvh, hd, t_pad)[:, :, :t]
    vt = _mm(wvt, ht, MXU_DTYPE, tm_prefs=(512, 256, 128), tn_prefs=TOKEN_TILES, name="proj_v")

    m_lanes = ATTN_LANES
    tq = m_lanes // g
    assert tc % tq == 0 and seq % tq == 0 and (not windowed or tq == Q_BLOCK)
    vt = vt.reshape(kvh, hd, t_pad)[:, :, :t]
    if sink is None:
        m_init = jnp.full((kvh, 1, m_lanes), NEG_INF, jnp.float32)
        init_l = 0.0
    else:
        m_init = jnp.repeat(sink.reshape(kvh, g).astype(jnp.float32) * LOG2E, tq, axis=1).reshape(kvh, 1, m_lanes)
        init_l = 1.0
    kc, vc = kk[:, :, :tc], vt[:, :, :tc]
    if windowed:
        va = _v_aug(vt)
        o_lat = _window(qt, kc, va[:, :, :tc], kk[:, :, tc:], va[:, :, tc:], m_init, tc // tq, tc, seq)
    else:
        tk = _pick(seq, FLASH_KEY_CHUNKS)
        kl = kk[:, :, tc:].reshape(kvh, hd, seq // tk, tk).transpose(0, 2, 1, 3)
        vl = vt[:, :, tc:].reshape(kvh, hd, seq // tk, tk).transpose(0, 2, 1, 3)
        o_lat = _flash(qt, kc, vc, kl, vl, m_init, init_l, tc // tq, seq // tq, name="global_attn")
    if with_ctx_out:
        o_ctx = _flash(qt, kc, vc, None, None, m_init, init_l, 0, tc // tq, name="ctx_attn")
    else:
        o_ctx = jnp.zeros((kvh, g, hd, tc), MXU_DTYPE)
    o_pad = jnp.zeros((kvh, g, hd, t_pad - t), MXU_DTYPE)
    ot = jnp.concatenate([o_ctx, o_lat, o_pad], axis=3).reshape(qd, t_pad)
    return _mm_res_t(w_o.T.astype(MXU_DTYPE), ot, xs, gates, tc, name="mixer_out_t")


def _dft_parts(n, scale):
    k = np.arange(n)
    ang = 2.0 * np.pi * ((k[:, None] * k[None, :]) % n) / n
    return np.cos(ang) * scale, -np.sin(ang) * scale


def _chan_dft_kernel(x_ref, c_ref, s_ref, o_ref):
    x = x_ref[...]
    o_ref[0] = jnp.dot(x, c_ref[...], preferred_element_type=jnp.float32).astype(o_ref.dtype)
    o_ref[1] = jnp.dot(x, s_ref[...], preferred_element_type=jnp.float32).astype(o_ref.dtype)


def _chan_dft(h, gc):
    t, d = h.shape
    re, im = _dft_parts(gc, gc ** -0.5)
    tm = _pick(t, ROW_TILES)
    return pl.pallas_call(
        _chan_dft_kernel,
        grid=(t // tm, d // gc),
        in_specs=[pl.BlockSpec((tm, gc), lambda i, j: (i, j)),
                  pl.BlockSpec((gc, gc), lambda i, j: (0, 0)),
                  pl.BlockSpec((gc, gc), lambda i, j: (0, 0))],
        out_specs=pl.BlockSpec((2, tm, gc), lambda i, j: (0, i, j)),
        out_shape=jax.ShapeDtypeStruct((2, t, d), MXU_DTYPE),
        compiler_params=_params("parallel", "parallel"),
        name="chan_dft",
    )(h, jnp.asarray(re, MXU_DTYPE), jnp.asarray(im, MXU_DTYPE))


def _pos_stage1_kernel(m_ref, y_ref, tr_ref, ti_ref, o_ref, *, n1):
    y = jnp.concatenate([y_ref[0], y_ref[1]], axis=0)
    a = jnp.dot(m_ref[...], y, preferred_element_type=jnp.float32)
    ar, ai = a[:n1], a[n1:]
    tr, ti = tr_ref[0][:, 0:1], ti_ref[0][:, 0:1]
    o_ref[0] = (ar * tr - ai * ti).astype(o_ref.dtype)
    o_ref[1] = (ar * ti + ai * tr).astype(o_ref.dtype)


def _pos_stage2_kernel(m_ref, b_ref, o_ref):
    b = jnp.concatenate([b_ref[0, 0], b_ref[1, 0]], axis=0)
    o_ref[0] = jnp.dot(m_ref[...], b, preferred_element_type=jnp.float32).astype(o_ref.dtype)


def _pos_dft_two_stage(y, n1, n2):
    _, s, d = y.shape
    sc = float(s) ** -0.25
    r1, i1 = _dft_parts(n1, sc)
    m1 = np.block([[r1, -i1], [i1, r1]])
    f1 = np.arange(n1)[None, :]
    t2 = np.arange(n2)[:, None]
    ang = 2.0 * np.pi * (t2 * f1) / s
    tw_r = np.broadcast_to(np.cos(ang)[:, :, None], (n2, n1, V7X_LANES))
    tw_i = np.broadcast_to(-np.sin(ang)[:, :, None], (n2, n1, V7X_LANES))
    y4 = y.reshape(2, n1, n2 * d)
    b = pl.pallas_call(
        functools.partial(_pos_stage1_kernel, n1=n1),
        grid=(n2,),
        in_specs=[pl.BlockSpec((2 * n1, 2 * n1), lambda j: (0, 0)),
                  pl.BlockSpec((2, n1, d), lambda j: (0, 0, j)),
                  pl.BlockSpec((1, n1, V7X_LANES), lambda j: (j, 0, 0)),
                  pl.BlockSpec((1, n1, V7X_LANES), lambda j: (j, 0, 0))],
        out_specs=pl.BlockSpec((2, n1, d), lambda j: (0, 0, j)),
        out_shape=jax.ShapeDtypeStruct((2, n1, n2 * d), MXU_DTYPE),
        compiler_params=_params("parallel"),
        name="pos_dft_stage1",
    )(jnp.asarray(m1, MXU_DTYPE), y4, jnp.asarray(tw_r, jnp.float32), jnp.asarray(tw_i, jnp.float32))
    b4 = b.reshape(2, n1, n2, d)
    r2, i2 = _dft_parts(n2, sc)
    m2 = np.concatenate([r2, -i2], axis=1)
    tcn = _pick(d, (2048, 1024, 512, 256, 128))
    out = pl.pallas_call(
        _pos_stage2_kernel,
        grid=(n1, d // tcn),
        in_specs=[pl.BlockSpec((n2, 2 * n2), lambda f, j: (0, 0)),
                  pl.BlockSpec((2, 1, n2, tcn), lambda f, j: (0, f, 0, j))],
        out_specs=pl.BlockSpec((1, n2, tcn), lambda f, j: (f, 0, j)),
        out_shape=jax.ShapeDtypeStruct((n1, n2, d), MXU_DTYPE),
        compiler_params=_params("parallel", "parallel"),
        name="pos_dft_stage2",
    )(jnp.asarray(m2, MXU_DTYPE), b4)
    return out.transpose(1, 0, 2).reshape(s, d)


def _fourier_mix(h, tc, seq):
    t_pad, d = h.shape
    y = _chan_dft(h, d // FNET_GROUPS)
    rc, ic = _dft_parts(tc, tc ** -0.5)
    mc = np.concatenate([rc, -ic], axis=1)
    yc = jnp.concatenate([y[0, :tc], y[1, :tc]], axis=0)
    mixed_c = _mm(jnp.asarray(mc, MXU_DTYPE), yc, MXU_DTYPE, name="ctx_pos_dft")
    n1 = 1 << (int(math.log2(seq)) // 2)
    n2 = seq // n1
    mixed_l = _pos_dft_two_stage(y[:, tc:tc + seq], n1, n2)
    return jnp.concatenate([mixed_c, mixed_l, jnp.zeros((t_pad - tc - seq, d), MXU_DTYPE)], axis=0)


def _top16(s):
    n = s.shape[0]
    rows = lax.broadcasted_iota(jnp.int32, s.shape, 0)
    row16 = lax.broadcasted_iota(jnp.int32, (PEER_TOPK, s.shape[1]), 0)
    rank = jnp.full(s.shape, RANK_NONE, jnp.float32)
    sv = jnp.zeros((PEER_TOPK, s.shape[1]), jnp.float32)
    v = s
    for k in range(PEER_TOPK):
        m = jnp.max(v, axis=0, keepdims=True)
        idx = jnp.min(jnp.where(v == m, rows, n), axis=0, keepdims=True)
        hit = rows == idx
        rank = jnp.where(hit, float(k), rank)
        v = jnp.where(hit, -jnp.inf, v)
        sv = jnp.where(row16 == k, m, sv)
    return rank, sv


_CAND_GROUPS = ((0, 0, 8), (0, 8, 8), (1, 0, 8), (2, 0, 5), (3, 0, 4), (4, 0, 3), (5, 0, 2), (6, 0, 2), (7, 0, 2))


def _select_pairs(sv0, sv1):
    lanes = sv0.shape[1]
    sub = lax.broadcasted_iota(jnp.int32, (V7X_SUBLANES, lanes), 0)
    cands, poss = [], []
    for r0, r1_0, cnt in _CAND_GROUPS:
        c = sv0[r0:r0 + 1] + sv1[r1_0:r1_0 + V7X_SUBLANES]
        cands.append(jnp.where(sub < cnt, c, -jnp.inf))
        poss.append(r0 * PEER_TOPK + r1_0 + sub)
    cands.append(sv0[V7X_SUBLANES:] + sv1[0:1])
    poss.append((V7X_SUBLANES + sub) * PEER_TOPK)
    cand = jnp.concatenate(cands, axis=0)
    pos = jnp.concatenate(poss, axis=0)
    cmax = sv0[0:1] + sv1[0:1]
    big = PEER_TOPK * PEER_TOPK
    c = cand
    sel = jnp.zeros(cand.shape, jnp.float32)
    for _ in range(PEER_TOPK):
        m = jnp.max(c, axis=0, keepdims=True)
        pidx = jnp.min(jnp.where(c == m, pos, big), axis=0, keepdims=True)
        hit = pos == pidx
        sel = jnp.where(hit, 1.0, sel)
        c = jnp.where(hit, -jnp.inf, c)
    z = jnp.sum(jnp.where(sel > 0.0, jnp.exp(cand - cmax), 0.0), axis=0, keepdims=True)
    g = V7X_SUBLANES
    rows = [jnp.sum(sel[0:2 * g], axis=0, keepdims=True)]
    for i in range(2, len(_CAND_GROUPS)):
        rows.append(jnp.sum(sel[i * g:(i + 1) * g], axis=0, keepdims=True))
    rows.append(sel[len(_CAND_GROUPS) * g:])
    return jnp.concatenate(rows, axis=0), z


def _peer_select_kernel(q_ref, keys_ref, r1_ref, b_ref, cnt_ref, a_ref):
    def head(hh, carry):
        base = pl.multiple_of(hh * 2 * PEER_HALF, 2 * PEER_HALF)
        q0 = q_ref[pl.ds(base, PEER_HALF), :]
        q1 = q_ref[pl.ds(base + PEER_HALF, PEER_HALF), :]
        s0 = jnp.dot(keys_ref[hh, 0], q0, preferred_element_type=jnp.float32, precision=lax.Precision.HIGHEST)
        s1 = jnp.dot(keys_ref[hh, 1], q1, preferred_element_type=jnp.float32, precision=lax.Precision.HIGHEST)
        rank0, sv0 = _top16(s0)
        rank1, sv1 = _top16(s1)
        cnt16, z = _select_pairs(sv0, sv1)
        cnt = jnp.zeros(s0.shape, jnp.float32)
        for r in range(PEER_TOPK):
            cnt = jnp.where(rank0 == float(r), cnt16[r:r + 1], cnt)
        r1_ref[hh] = rank1
        b_ref[hh] = jnp.exp(s1 - sv1[0:1])
        cnt_ref[hh] = cnt
        a_ref[hh] = jnp.exp(s0 - sv0[0:1]) / z
        return carry

    lax.fori_loop(0, PEER_HEADS, head, 0, unroll=4)


def _peer_select(qt, sub_keys):
    _, t = qt.shape
    tl = _pick(t, (2 * V7X_LANES, V7X_LANES))
    shp = jax.ShapeDtypeStruct((PEER_HEADS, PEER_N_KEYS, t), jnp.float32)
    ospec = pl.BlockSpec((PEER_HEADS, PEER_N_KEYS, tl), lambda i: (0, 0, i))
    return pl.pallas_call(
        _peer_select_kernel,
        grid=(t // tl,),
        in_specs=[pl.BlockSpec((qt.shape[0], tl), lambda i: (0, i)),
                  pl.BlockSpec(sub_keys.shape, lambda i: (0, 0, 0, 0))],
        out_specs=[ospec] * 4,
        out_shape=[shp] * 4,
        compiler_params=_params("parallel"),
        name="peer_select",
    )(qt, sub_keys)


def _gelu_exact(x):
    return 0.5 * x * (1.0 + lax.erf(x * (2.0 ** -0.5)))


def _peer_act_kernel(u_ref, h_ref, r1_ref, b_ref, cnt_ref, a_ref, o_ref, *, iblocks):
    nk = PEER_N_KEYS
    groups = [slice(g * MXU_COLS, (g + 1) * MXU_COLS) for g in range(h_ref.shape[1] // MXU_COLS)]
    pres = [jnp.dot(u_ref[...], h_ref[:, cols], preferred_element_type=jnp.float32) for cols in groups]
    dt = o_ref.dtype
    for pre, cols in zip(pres, groups):
        r1 = [r1_ref[hh, :, cols].astype(dt) for hh in range(PEER_HEADS)]
        b = [b_ref[hh, :, cols].astype(dt) for hh in range(PEER_HEADS)]
        for ii in range(iblocks):
            w = jnp.zeros((nk, MXU_COLS), dt)
            for hh in range(PEER_HEADS):
                sel = r1[hh] < cnt_ref[hh, ii:ii + 1, cols].astype(dt)
                w = w + jnp.where(sel, b[hh], jnp.zeros_like(b[hh])) * a_ref[hh, ii:ii + 1, cols].astype(dt)
            o_ref[ii * nk:(ii + 1) * nk, cols] = _gelu_exact(pre[ii * nk:(ii + 1) * nk]).astype(dt) * w


def _peer_act(u, layer, ht, r1, b, cnt, a):
    _, e, d = u.shape
    _, t = ht.shape
    iblocks = V7X_SUBLANES
    te = iblocks * PEER_N_KEYS
    tt = _pick(t, TOKEN_TILES)
    full = pl.BlockSpec((PEER_HEADS, PEER_N_KEYS, tt), lambda i, j: (0, 0, i))
    part = pl.BlockSpec((PEER_HEADS, iblocks, tt), lambda i, j: (0, j, i))
    return pl.pallas_call(
        functools.partial(_peer_act_kernel, iblocks=iblocks),
        grid=(t // tt, e // te),
        in_specs=[pl.BlockSpec((None, te, d), lambda i, j: (layer, j, 0)),
                  pl.BlockSpec((d, tt), lambda i, j: (0, i)),
                  full, full, part, part],
        out_specs=pl.BlockSpec((te, tt), lambda i, j: (j, i)),
        out_shape=jax.ShapeDtypeStruct((e, t), MXU_DTYPE),
        compiler_params=_params("parallel", "parallel"),
        name="peer_act",
    )(u, ht, r1, b, cnt, a)


def _peer(x, nw, mod_ctx, mod_lat, gates, tc, w_q, sub_keys, u_all, vt_all, layer):
    (ht,) = _modulate(x, nw, mod_ctx, mod_lat, tc, ("feature_major",))
    qt = _mm(w_q.T.astype(MXU_DTYPE), ht, jnp.float32, tm_prefs=(512, 256, 128),
             tn_prefs=TOKEN_TILES, name="peer_query")
    r1, b, cnt, a = _peer_select(qt, sub_keys)
    act = _peer_act(u_all, layer, ht, r1, b, cnt, a)
    return _mm_res_t(vt_all, act, x, gates, tc, layer=layer, name="peer_out")


def kernel(x, c, ctx, c_ctx, ada_w, ada_b, mix_norm, ffn_norm, a_w_qkv, a_w_o, a_q_norm, a_k_norm,
           b_w_qkv, b_w_o, b_q_norm, b_k_norm, b_sink, f_w_out, peer_w_q, peer_sub_keys, peer_u, peer_v):
    batch, seq, d = x.shape
    assert batch == 1 and ctx.shape[0] == 1
    tc = ctx.shape[1]
    depth = ada_w.shape[0]
    pad = -(tc + seq) % TOKEN_TILES[0]
    xs = jnp.concatenate([ctx[0], x[0], jnp.zeros((pad, d), x.dtype)], axis=0)
    mods = _ada(jnp.stack([c[0], c_ctx]), ada_w, ada_b)
    mods = mods.reshape(depth, 2, N_MOD, d)
    u_all = peer_u.astype(MXU_DTYPE)
    vt_all = peer_v.transpose(0, 2, 1).astype(MXU_DTYPE)
    for layer in range(depth):
        last = layer == depth - 1
        kind, j = layer % N_MIXERS, layer // N_MIXERS
        lat, cx = mods[layer, 0], mods[layer, 1]
        gates = jnp.stack([cx[2], lat[2]])
        if kind == 2:
            (h,) = _modulate(xs, mix_norm[layer], cx[0:2], lat[0:2], tc, ("token_major",))
            xs = _mm_res(_fourier_mix(h, tc, seq), f_w_out[j].astype(MXU_DTYPE), xs, gates, tc, name="mixer_out")
        else:
            (ht,) = _modulate(xs, mix_norm[layer], cx[0:2], lat[0:2], tc, ("feature_major",))
            if kind == 0:
                xs = _attention_layer(xs, ht, gates, a_w_qkv[j], a_w_o[j], a_q_norm[j], a_k_norm[j], None,
                                      A_HEAD_DIM, A_GROUP, tc, seq, False, not last)
            else:
                xs = _attention_layer(xs, ht, gates, b_w_qkv[j], b_w_o[j], b_q_norm[j], b_k_norm[j], b_sink[j],
                                      B_HEAD_DIM, B_GROUP, tc, seq, True, not last)
        xs = _peer(xs, ffn_norm[layer], cx[3:5], lat[3:5], jnp.stack([cx[5], lat[5]]), tc,
                   peer_w_q[layer], peer_sub_keys[layer], u_all, vt_all, layer)
    return xs[tc:tc + seq][None]
```

```python
import functools
import math

import numpy as np
import jax
import jax.numpy as jnp
from jax import lax
from jax.experimental import pallas as pl
from jax.experimental.pallas import tpu as pltpu

GRID_W = 64
Q_BLOCK = 128
ROPE_THETA = 10000.0
NORM_EPS = 1e-6
NEG_INF = -1e30
N_MOD = 6
N_MIXERS = 3
A_HEAD_DIM = 128
A_GROUP = 4
B_HEAD_DIM = 64
B_GROUP = 8
WINDOW = 128
FNET_GROUPS = 8
PEER_HEADS = 8
PEER_N_KEYS = 96
PEER_TOPK = 16
PEER_HALF = 128

V7X_LANES = 128
V7X_SUBLANES = 8
V7X_BF16_ROWS = 16
MXU_COLS = 256
V7X_VMEM_BYTES = 64 * 1024 * 1024
VMEM_LIMIT = V7X_VMEM_BYTES - 8 * 1024 * 1024

MXU_DTYPE = jnp.bfloat16
LOG2E = 1.4426950408889634
RANK_NONE = 99.0
ROW_TILES = (1536, 768, 512, 256, 128)
TOKEN_TILES = (768, 512, 256)
ATTN_LANES = 1024
FLASH_KEY_CHUNKS = (4096, 2048, 1024, 512, 256)
EXP2_HEADROOM = 64.0


def _params(*sem):
    return pltpu.CompilerParams(dimension_semantics=sem, vmem_limit_bytes=VMEM_LIMIT)


def _pick(n, prefs):
    for p in prefs:
        if n % p == 0:
            return p
    return n


def _ada_kernel(cond_ref, w_ref, b_ref, o_ref):
    w = w_ref[0]
    reps = w.shape[1] // V7X_LANES
    rows = []
    for r in range(2):
        c = cond_ref[r]
        s = c * jax.nn.sigmoid(c)
        rows.append(jnp.sum(w * jnp.tile(s, (1, reps)), axis=0, keepdims=True))
    o_ref[0] = jnp.concatenate(rows, axis=0) + b_ref[0]


def _ada(cond2, ada_w, ada_b):
    depth, d, n = ada_w.shape
    tn = _pick(n, (512, 256, 128))
    cond_rep = jnp.broadcast_to(cond2[:, :, None], (2, d, V7X_LANES))
    return pl.pallas_call(
        _ada_kernel,
        grid=(depth, n // tn),
        in_specs=[pl.BlockSpec((2, d, V7X_LANES), lambda l, j: (0, 0, 0)),
                  pl.BlockSpec((1, d, tn), lambda l, j: (l, 0, j)),
                  pl.BlockSpec((1, 1, tn), lambda l, j: (l, 0, j))],
        out_specs=pl.BlockSpec((1, 2, tn), lambda l, j: (l, 0, j)),
        out_shape=jax.ShapeDtypeStruct((depth, 2, n), jnp.float32),
        compiler_params=_params("parallel", "parallel"),
        name="ada",
    )(cond_rep, ada_w, ada_b.reshape(depth, 1, n))


def _modulate_kernel(x_ref, nw_ref, mod_ref, *o_refs, layouts):
    x = x_ref[...]
    y = x * lax.rsqrt(jnp.mean(x * x, axis=-1, keepdims=True) + NORM_EPS) * nw_ref[...]
    shift = mod_ref[0, 0:1, :]
    scale = mod_ref[0, 1:2, :]
    h = y * (1.0 + scale) + shift
    for o_ref, layout in zip(o_refs, layouts):
        o_ref[...] = (h.T if layout == "feature_major" else h).astype(o_ref.dtype)


def _modulate(x, nw, mod_ctx, mod_lat, tc, layouts):
    t, d = x.shape
    tm = _pick(math.gcd(t, tc), (256, 128))
    ctx_tiles = tc // tm
    mods = jnp.stack([mod_ctx, mod_lat])
    shapes = [jax.ShapeDtypeStruct((d, t) if lay == "feature_major" else (t, d), MXU_DTYPE) for lay in layouts]
    specs = [pl.BlockSpec((d, tm), lambda i: (0, i)) if lay == "feature_major"
             else pl.BlockSpec((tm, d), lambda i: (i, 0)) for lay in layouts]
    return pl.pallas_call(
        functools.partial(_modulate_kernel, layouts=tuple(layouts)),
        grid=(t // tm,),
        in_specs=[pl.BlockSpec((tm, d), lambda i: (i, 0)),
                  pl.BlockSpec((1, d), lambda i: (0, 0)),
                  pl.BlockSpec((1, 2, d), lambda i: (jnp.where(i < ctx_tiles, 0, 1), 0, 0))],
        out_specs=specs,
        out_shape=shapes,
        compiler_params=_params("parallel"),
        name="modulate",
    )(x, nw.reshape(1, d), mods)


def _mm_kernel(a_ref, b_ref, o_ref):
    o_ref[...] = jnp.dot(a_ref[...], b_ref[...], preferred_element_type=jnp.float32).astype(o_ref.dtype)


def _mm(a, b, out_dtype, *, tm_prefs=ROW_TILES, tn_prefs=(512, 256, 128), name="mm"):
    m, k = a.shape
    _, n = b.shape
    tm, tn = _pick(m, tm_prefs), _pick(n, tn_prefs)
    return pl.pallas_call(
        _mm_kernel,
        grid=(m // tm, n // tn),
        in_specs=[pl.BlockSpec((tm, k), lambda i, j: (i, 0)),
                  pl.BlockSpec((k, tn), lambda i, j: (0, j))],
        out_specs=pl.BlockSpec((tm, tn), lambda i, j: (i, j)),
        out_shape=jax.ShapeDtypeStruct((m, n), out_dtype),
        compiler_params=_params("parallel", "parallel"),
        name=name,
    )(a, b)


def _mm_res_kernel(a_ref, b_ref, res_ref, g_ref, o_ref, *, tc, transposed):
    acc = jnp.dot(a_ref[...], b_ref[...], preferred_element_type=jnp.float32)
    if transposed:
        acc = acc.T
        row0 = pl.program_id(0) * acc.shape[0]
    else:
        row0 = pl.program_id(0) * acc.shape[0]
    rid = row0 + lax.broadcasted_iota(jnp.int32, acc.shape, 0)
    gate = jnp.where(rid < tc, g_ref[0:1, :], g_ref[1:2, :])
    o_ref[...] = res_ref[...] + gate * acc


def _mm_res(a, b, res, gates, tc, *, name="mm_res"):
    m, k = a.shape
    _, n = b.shape
    tm, tn = _pick(m, ROW_TILES), _pick(n, (512, 256, 128))
    return pl.pallas_call(
        functools.partial(_mm_res_kernel, tc=tc, transposed=False),
        grid=(m // tm, n // tn),
        in_specs=[pl.BlockSpec((tm, k), lambda i, j: (i, 0)),
                  pl.BlockSpec((k, tn), lambda i, j: (0, j)),
                  pl.BlockSpec((tm, tn), lambda i, j: (i, j)),
                  pl.BlockSpec((2, tn), lambda i, j: (0, j))],
        out_specs=pl.BlockSpec((tm, tn), lambda i, j: (i, j)),
        out_shape=jax.ShapeDtypeStruct((m, n), jnp.float32),
        compiler_params=_params("parallel", "parallel"),
        name=name,
    )(a, b, res, gates)


def _mm_res_t(vt, at, res, gates, tc, *, layer=None, name="mm_res_t"):
    n, k = vt.shape[-2:]
    _, t = at.shape
    vspec = (lambda tn: pl.BlockSpec((tn, k), lambda i, j: (j, 0))) if layer is None else (
        lambda tn: pl.BlockSpec((None, tn, k), lambda i, j: (layer, j, 0)))
    tt, tn = _pick(t, TOKEN_TILES), _pick(n, (512, 256, 128) if k <= 4096 else (256, 128))
    return pl.pallas_call(
        functools.partial(_mm_res_kernel, tc=tc, transposed=True),
        grid=(t // tt, n // tn),
        in_specs=[vspec(tn),
                  pl.BlockSpec((k, tt), lambda i, j: (0, i)),
                  pl.BlockSpec((tt, tn), lambda i, j: (i, j)),
                  pl.BlockSpec((2, tn), lambda i, j: (0, j))],
        out_specs=pl.BlockSpec((tt, tn), lambda i, j: (i, j)),
        out_shape=jax.ShapeDtypeStruct((t, n), jnp.float32),
        compiler_params=_params("parallel", "parallel"),
        name=name,
    )(vt, at, res, gates)


def _mm_qk_t_kernel(w_ref, h_ref, nw_ref, cos_ref, sin_ref, o_ref, *, hd):
    acc = jnp.dot(w_ref[...], h_ref[...], preferred_element_type=jnp.float32)
    cos = cos_ref[...]
    sin = sin_ref[...]
    reps = acc.shape[1] // V7X_LANES
    half = hd // 2
    for r in range(acc.shape[0] // hd):
        x = acc[r * hd:(r + 1) * hd]
        ms = jnp.sum(x * x, axis=0, keepdims=True) * (1.0 / hd)
        y = x * lax.rsqrt(ms + NORM_EPS) * jnp.tile(nw_ref[r * hd:(r + 1) * hd], (1, reps))
        y1, y2 = y[:half], y[half:]
        o_ref[r * hd:r * hd + half] = (y1 * cos - y2 * sin).astype(o_ref.dtype)
        o_ref[r * hd + half:(r + 1) * hd] = (y1 * sin + y2 * cos).astype(o_ref.dtype)


def _mm_qk_t(wt, ht, nw, cos, sin, hd, *, name):
    n, k = wt.shape
    _, t = ht.shape
    tf, tt = _pick(n, (512, 256, 128)), _pick(t, TOKEN_TILES)
    nw_rep = jnp.broadcast_to(nw[:, None], (n, V7X_LANES))
    return pl.pallas_call(
        functools.partial(_mm_qk_t_kernel, hd=hd),
        grid=(t // tt, n // tf),
        in_specs=[pl.BlockSpec((tf, k), lambda i, j: (j, 0)),
                  pl.BlockSpec((k, tt), lambda i, j: (0, i)),
                  pl.BlockSpec((tf, V7X_LANES), lambda i, j: (j, 0)),
                  pl.BlockSpec((hd // 2, tt), lambda i, j: (0, i)),
                  pl.BlockSpec((hd // 2, tt), lambda i, j: (0, i))],
        out_specs=pl.BlockSpec((tf, tt), lambda i, j: (j, i)),
        out_shape=jax.ShapeDtypeStruct((n, t), MXU_DTYPE),
        compiler_params=_params("parallel", "parallel"),
        name=name,
    )(wt, ht, nw_rep, cos, sin)


def _flash_kernel(q_ref, kc_ref, vc_ref, *rest, init_l, n_lat, tk):
    if n_lat:
        kl_ref, vl_ref, m0_ref, o_ref, mrun_ref, macc_ref, l_ref, acc_ref, knorm_ref = rest
    else:
        m0_ref, o_ref, mrun_ref, macc_ref, l_ref, acc_ref, knorm_ref = rest
    q = jnp.concatenate([q_ref[0, g] for g in range(q_ref.shape[1])], axis=1)

    def chunk(ref, c):
        return ref[0, :, pl.ds(pl.multiple_of(c * tk, tk), tk)]

    def norm2(k):
        kf = k.astype(jnp.float32)
        return jnp.max(jnp.sum(kf * kf, axis=0, keepdims=True), axis=1, keepdims=True)

    if n_lat:
        @pl.when(pl.program_id(1) == 0)
        def _():
            k2 = lax.fori_loop(0, n_lat, lambda c, mx: jnp.maximum(mx, norm2(chunk(kl_ref, c))), norm2(kc_ref[0]))
            knorm_ref[...] = jnp.broadcast_to(jnp.sqrt(k2), knorm_ref.shape)

    mrun_ref[...] = m0_ref[0]
    macc_ref[...] = m0_ref[0]
    l_ref[...] = jnp.full(l_ref.shape, init_l, jnp.float32)
    acc_ref[...] = jnp.zeros(acc_ref.shape, jnp.float32)

    def accumulate(p, v, m_to):
        alpha = jnp.exp2(macc_ref[...] - m_to)
        l_ref[...] = l_ref[...] * alpha + jnp.sum(p, axis=0, keepdims=True)
        acc_ref[...] = acc_ref[...] * alpha + jnp.dot(v, p.astype(MXU_DTYPE), preferred_element_type=jnp.float32)
        macc_ref[...] = m_to

    def scores(k):
        return lax.dot_general(k, q, (((0,), (0,)), ((), ())), preferred_element_type=jnp.float32)

    def own_max(k, v):
        s = scores(k)
        m_new = jnp.maximum(mrun_ref[...], jnp.max(s, axis=0, keepdims=True))
        accumulate(jnp.exp2(s - m_new), v, m_new)
        mrun_ref[...] = m_new

    def earlier_max(k, v):
        m_use = mrun_ref[...]
        s = scores(k)
        accumulate(jnp.exp2(s - m_use), v, m_use)
        mrun_ref[...] = jnp.maximum(m_use, jnp.max(s, axis=0, keepdims=True))

    own_max(kc_ref[0], vc_ref[0])
    if n_lat:
        qf = q.astype(jnp.float32)
        qnorm = jnp.sqrt(jnp.sum(qf * qf, axis=0, keepdims=True))
        bounded = jnp.max(qnorm * knorm_ref[...] - mrun_ref[...]) <= EXP2_HEADROOM

        def loop(step):
            def body(c, carry):
                step(chunk(kl_ref, c), chunk(vl_ref, c))
                return carry
            lax.fori_loop(0, n_lat, body, 0)

        pl.when(bounded)(lambda: loop(earlier_max))
        pl.when(jnp.logical_not(bounded))(lambda: loop(own_max))

    o = (acc_ref[...] / l_ref[...]).astype(o_ref.dtype)
    tq = o_ref.shape[3]
    for g in range(o_ref.shape[1]):
        o_ref[0, g] = o[:, g * tq:(g + 1) * tq]


def _flash(qt, kc, vc, kl, vl, m0, init_l, first_tile, n_tiles, *, name):
    kvh, g, hd, _ = qt.shape
    tc = kc.shape[2]
    m_lanes = m0.shape[2]
    tq = m_lanes // g
    n_lat, tk = 0, 0
    lat_specs, lat_args = [], []
    if kl is not None:
        seq = kl.shape[2]
        tk = _pick(seq, FLASH_KEY_CHUNKS)
        n_lat = seq // tk
        lat_specs = [pl.BlockSpec((1, hd, seq), lambda h, i: (h, 0, 0))] * 2
        lat_args = [kl, vl]
    return pl.pallas_call(
        functools.partial(_flash_kernel, init_l=init_l, n_lat=n_lat, tk=tk),
        grid=(kvh, n_tiles),
        in_specs=[pl.BlockSpec((1, g, hd, tq), lambda h, i: (h, 0, 0, i + first_tile)),
                  pl.BlockSpec((1, hd, tc), lambda h, i: (h, 0, 0)),
                  pl.BlockSpec((1, hd, tc), lambda h, i: (h, 0, 0)),
                  *lat_specs,
                  pl.BlockSpec((1, 1, m_lanes), lambda h, i: (h, 0, 0))],
        out_specs=pl.BlockSpec((1, g, hd, tq), lambda h, i: (h, 0, 0, i)),
        out_shape=jax.ShapeDtypeStruct((kvh, g, hd, n_tiles * tq), MXU_DTYPE),
        scratch_shapes=[pltpu.VMEM((1, m_lanes), jnp.float32),
                        pltpu.VMEM((1, m_lanes), jnp.float32),
                        pltpu.VMEM((1, m_lanes), jnp.float32),
                        pltpu.VMEM((hd, m_lanes), jnp.float32),
                        pltpu.VMEM((1, m_lanes), jnp.float32)],
        compiler_params=_params("parallel", "arbitrary"),
        name=name,
    )(qt, kc, vc, *lat_args, m0)


def _window_kernel(q_ref, kc_ref, vc_ref, kp_ref, kb_ref, kn_ref, vp_ref, vb_ref, vn_ref, bias_ref, sink_ref,
                   o_ref, *, hd):
    q = jnp.concatenate([q_ref[0, g] for g in range(q_ref.shape[1])], axis=1)
    kcat = jnp.concatenate([kc_ref[0], kp_ref[0], kb_ref[0], kn_ref[0]], axis=1)
    vcat = jnp.concatenate([vc_ref[0], vp_ref[0], vb_ref[0], vn_ref[0]], axis=1)
    s = lax.dot_general(kcat, q, (((0,), (0,)), ((), ())), preferred_element_type=jnp.float32) + bias_ref[0]
    sink = sink_ref[0]
    m = jnp.maximum(sink, jnp.max(s, axis=0, keepdims=True))
    p = jnp.exp2(s - m).astype(MXU_DTYPE)
    acc = jnp.dot(vcat, p, preferred_element_type=jnp.float32)
    l = acc[hd:hd + 1] + jnp.exp2(sink - m)
    o = (acc[:hd] / l).astype(o_ref.dtype)
    for g in range(o_ref.shape[1]):
        o_ref[0, g] = o[:, g * Q_BLOCK:(g + 1) * Q_BLOCK]


def _window_bias(tc, g):
    key_off = jnp.arange(-WINDOW, 2 * Q_BLOCK)[:, None]
    q_off = jnp.arange(Q_BLOCK)[None, :]
    band = jnp.abs(q_off - key_off) <= WINDOW
    variants = [band & (key_off >= 0), band, band & (key_off < Q_BLOCK)]
    lat = jnp.stack([jnp.where(v, 0.0, NEG_INF).astype(jnp.float32) for v in variants])
    lat = jnp.tile(lat, (1, 1, g))
    return jnp.concatenate([jnp.zeros((3, tc, g * Q_BLOCK), jnp.float32), lat], axis=1)


def _window(qt, kc, vc, kl, vl, sink_m, first_tile, tc, seq):
    kvh, g, hd, _ = qt.shape
    hv = vc.shape[1]
    m_lanes = g * Q_BLOCK
    nb = seq // Q_BLOCK
    assert nb >= 2
    nkeys = tc + 3 * Q_BLOCK
    kspec = lambda f: pl.BlockSpec((1, hd, Q_BLOCK), lambda h, b: (h, 0, f(b)))
    vspec = lambda f: pl.BlockSpec((1, hv, Q_BLOCK), lambda h, b: (h, 0, f(b)))
    prev = lambda b: jnp.maximum(b - 1, 0)
    nxt = lambda b: jnp.minimum(b + 1, nb - 1)
    cur = lambda b: b
    variant = lambda b: jnp.where(b == 0, 0, jnp.where(b == nb - 1, 2, 1))
    return pl.pallas_call(
        functools.partial(_window_kernel, hd=hd),
        grid=(kvh, nb),
        in_specs=[pl.BlockSpec((1, g, hd, Q_BLOCK), lambda h, b: (h, 0, 0, b + first_tile)),
                  pl.BlockSpec((1, hd, tc), lambda h, b: (h, 0, 0)),
                  pl.BlockSpec((1, hv, tc), lambda h, b: (h, 0, 0)),
                  kspec(prev), kspec(cur), kspec(nxt), vspec(prev), vspec(cur), vspec(nxt),
                  pl.BlockSpec((1, nkeys, m_lanes), lambda h, b: (variant(b), 0, 0)),
                  pl.BlockSpec((1, 1, m_lanes), lambda h, b: (h, 0, 0))],
        out_specs=pl.BlockSpec((1, g, hd, Q_BLOCK), lambda h, b: (h, 0, 0, b)),
        out_shape=jax.ShapeDtypeStruct((kvh, g, hd, seq), MXU_DTYPE),
        compiler_params=_params("parallel", "parallel"),
        name="window_attn",
    )(qt, kc, vc, kl, kl, kl, vl, vl, vl, _window_bias(tc, g), sink_m)


def _rope_angles(seq, hd):
    d_axis = hd // 2
    inv_freq = ROPE_THETA ** (-jnp.arange(0, d_axis, 2, dtype=jnp.float32) / d_axis)
    t = jnp.arange(seq)
    row = (t // GRID_W).astype(jnp.float32)[:, None] * inv_freq
    col = (t % GRID_W).astype(jnp.float32)[:, None] * inv_freq
    return jnp.concatenate([row, col], axis=1)


def _rope_tables_t(seq, tc, pad, hd):
    ang = _rope_angles(seq, hd).T
    cos = jnp.concatenate([jnp.ones((hd // 2, tc), jnp.float32), jnp.cos(ang),
                           jnp.ones((hd // 2, pad), jnp.float32)], axis=1)
    sin = jnp.concatenate([jnp.zeros((hd // 2, tc), jnp.float32), jnp.sin(ang),
                           jnp.zeros((hd // 2, pad), jnp.float32)], axis=1)
    return cos, sin


def _head_perm(hd):
    f = hd // 4
    idx = np.arange(hd).reshape(2, 2, f)
    return idx.transpose(1, 0, 2).reshape(hd)


def _v_aug(vt):
    kvh, _, n = vt.shape
    ones = jnp.ones((kvh, 1, n), vt.dtype)
    zeros = jnp.zeros((kvh, V7X_BF16_ROWS - 1, n), vt.dtype)
    return jnp.concatenate([vt, ones, zeros], axis=1)


def _attention_layer(xs, ht, gates, w_qkv, w_o, q_norm, k_norm, sink, hd, g, tc, seq, windowed, with_ctx_out):
    d, t_pad = ht.shape
    t = tc + seq
    nheads = d // hd
    kvh = nheads // g
    qd, kd = nheads * hd, kvh * hd
    perm = _head_perm(hd)
    wqt = w_qkv[:, :qd].T.reshape(nheads, hd, d)[:, perm].reshape(qd, d).astype(MXU_DTYPE)
    wkt = w_qkv[:, qd:qd + kd].T.reshape(kvh, hd, d)[:, perm].reshape(kd, d).astype(MXU_DTYPE)
    wvt = w_qkv[:, qd + kd:].T.astype(MXU_DTYPE)
    qscale = (hd ** -0.5) * LOG2E
    nwq = jnp.tile(q_norm[perm] * qscale, nheads)
    nwk = jnp.tile(k_norm[perm], kvh)
    cos_t, sin_t = _rope_tables_t(seq, tc, t_pad - t, hd)
    qt = _mm_qk_t(wqt, ht, nwq, cos_t, sin_t, hd, name="proj_q").reshape(kvh, g, hd, t_pad)
    kk = _mm_qk_t(wkt, ht, nwk, cos_t, sin_t, hd, name="proj_k").reshape(kvh, hd, t_pad)[:, :, :t]
    vt = _mm(wvt, ht, MXU_DTYPE, tm_prefs=(512, 256, 128), tn_prefs=TOKEN_TILES, name="proj_v")

    m_lanes = ATTN_LANES
    tq = m_lanes // g
    assert tc % tq == 0 and seq % tq == 0 and (not windowed or tq == Q_BLOCK)
    vt = vt.reshape(kvh, hd, t_pad)[:, :, :t]
    if sink is None:
        m_init = jnp.full((kvh, 1, m_lanes), NEG_INF, jnp.float32)
        init_l = 0.0
    else:
        m_init = jnp.repeat(sink.reshape(kvh, g).astype(jnp.float32) * LOG2E, tq, axis=1).reshape(kvh, 1, m_lanes)
        init_l = 1.0
    kc, vc = kk[:, :, :tc], vt[:, :, :tc]
    if windowed:
        va = _v_aug(vt)
        o_lat = _window(qt, kc, va[:, :, :tc], kk[:, :, tc:], va[:, :, tc:], m_init, tc // tq, tc, seq)
    else:
        o_lat = _flash(qt, kc, vc, kk[:, :, tc:], vt[:, :, tc:], m_init, init_l, tc // tq, seq // tq,
                       name="global_attn")
    if with_ctx_out:
        o_ctx = _flash(qt, kc, vc, None, None, m_init, init_l, 0, tc // tq, name="ctx_attn")
    else:
        o_ctx = jnp.zeros((kvh, g, hd, tc), MXU_DTYPE)
    o_pad = jnp.zeros((kvh, g, hd, t_pad - t), MXU_DTYPE)
    ot = jnp.concatenate([o_ctx, o_lat, o_pad], axis=3).reshape(qd, t_pad)
    return _mm_res_t(w_o.T.astype(MXU_DTYPE), ot, xs, gates, tc, name="mixer_out_t")


def _dft_parts(n, scale):
    k = np.arange(n)
    ang = 2.0 * np.pi * ((k[:, None] * k[None, :]) % n) / n
    return np.cos(ang) * scale, -np.sin(ang) * scale


def _chan_dft_kernel(x_ref, c_ref, s_ref, o_ref):
    x = x_ref[...]
    o_ref[0] = jnp.dot(x, c_ref[...], preferred_element_type=jnp.float32).astype(o_ref.dtype)
    o_ref[1] = jnp.dot(x, s_ref[...], preferred_element_type=jnp.float32).astype(o_ref.dtype)


def _chan_dft(h, gc):
    t, d = h.shape
    re, im = _dft_parts(gc, gc ** -0.5)
    tm = _pick(t, ROW_TILES)
    return pl.pallas_call(
        _chan_dft_kernel,
        grid=(t // tm, d // gc),
        in_specs=[pl.BlockSpec((tm, gc), lambda i, j: (i, j)),
                  pl.BlockSpec((gc, gc), lambda i, j: (0, 0)),
                  pl.BlockSpec((gc, gc), lambda i, j: (0, 0))],
        out_specs=pl.BlockSpec((2, tm, gc), lambda i, j: (0, i, j)),
        out_shape=jax.ShapeDtypeStruct((2, t, d), MXU_DTYPE),
        compiler_params=_params("parallel", "parallel"),
        name="chan_dft",
    )(h, jnp.asarray(re, MXU_DTYPE), jnp.asarray(im, MXU_DTYPE))


def _pos_stage1_kernel(m_ref, y_ref, tr_ref, ti_ref, o_ref, *, n1):
    y = jnp.concatenate([y_ref[0], y_ref[1]], axis=0)
    a = jnp.dot(m_ref[...], y, preferred_element_type=jnp.float32)
    ar, ai = a[:n1], a[n1:]
    tr, ti = tr_ref[0][:, 0:1], ti_ref[0][:, 0:1]
    o_ref[0] = (ar * tr - ai * ti).astype(o_ref.dtype)
    o_ref[1] = (ar * ti + ai * tr).astype(o_ref.dtype)


def _pos_stage2_kernel(m_ref, b_ref, o_ref):
    b = jnp.concatenate([b_ref[0, 0], b_ref[1, 0]], axis=0)
    o_ref[0] = jnp.dot(m_ref[...], b, preferred_element_type=jnp.float32).astype(o_ref.dtype)


def _pos_dft_two_stage(y, n1, n2):
    _, s, d = y.shape
    sc = float(s) ** -0.25
    r1, i1 = _dft_parts(n1, sc)
    m1 = np.block([[r1, -i1], [i1, r1]])
    f1 = np.arange(n1)[None, :]
    t2 = np.arange(n2)[:, None]
    ang = 2.0 * np.pi * (t2 * f1) / s
    tw_r = np.broadcast_to(np.cos(ang)[:, :, None], (n2, n1, V7X_LANES))
    tw_i = np.broadcast_to(-np.sin(ang)[:, :, None], (n2, n1, V7X_LANES))
    y4 = y.reshape(2, n1, n2 * d)
    b = pl.pallas_call(
        functools.partial(_pos_stage1_kernel, n1=n1),
        grid=(n2,),
        in_specs=[pl.BlockSpec((2 * n1, 2 * n1), lambda j: (0, 0)),
                  pl.BlockSpec((2, n1, d), lambda j: (0, 0, j)),
                  pl.BlockSpec((1, n1, V7X_LANES), lambda j: (j, 0, 0)),
                  pl.BlockSpec((1, n1, V7X_LANES), lambda j: (j, 0, 0))],
        out_specs=pl.BlockSpec((2, n1, d), lambda j: (0, 0, j)),
        out_shape=jax.ShapeDtypeStruct((2, n1, n2 * d), MXU_DTYPE),
        compiler_params=_params("parallel"),
        name="pos_dft_stage1",
    )(jnp.asarray(m1, MXU_DTYPE), y4, jnp.asarray(tw_r, jnp.float32), jnp.asarray(tw_i, jnp.float32))
    b4 = b.reshape(2, n1, n2, d)
    r2, i2 = _dft_parts(n2, sc)
    m2 = np.concatenate([r2, -i2], axis=1)
    tcn = _pick(d, (2048, 1024, 512, 256, 128))
    out = pl.pallas_call(
        _pos_stage2_kernel,
        grid=(n1, d // tcn),
        in_specs=[pl.BlockSpec((n2, 2 * n2), lambda f, j: (0, 0)),
                  pl.BlockSpec((2, 1, n2, tcn), lambda f, j: (0, f, 0, j))],
        out_specs=pl.BlockSpec((1, n2, tcn), lambda f, j: (f, 0, j)),
        out_shape=jax.ShapeDtypeStruct((n1, n2, d), MXU_DTYPE),
        compiler_params=_params("parallel", "parallel"),
        name="pos_dft_stage2",
    )(jnp.asarray(m2, MXU_DTYPE), b4)
    return out.transpose(1, 0, 2).reshape(s, d)


def _fourier_mix(h, tc, seq):
    t_pad, d = h.shape
    y = _chan_dft(h, d // FNET_GROUPS)
    rc, ic = _dft_parts(tc, tc ** -0.5)
    mc = np.concatenate([rc, -ic], axis=1)
    yc = jnp.concatenate([y[0, :tc], y[1, :tc]], axis=0)
    mixed_c = _mm(jnp.asarray(mc, MXU_DTYPE), yc, MXU_DTYPE, name="ctx_pos_dft")
    n1 = 1 << (int(math.log2(seq)) // 2)
    n2 = seq // n1
    mixed_l = _pos_dft_two_stage(y[:, tc:tc + seq], n1, n2)
    return jnp.concatenate([mixed_c, mixed_l, jnp.zeros((t_pad - tc - seq, d), MXU_DTYPE)], axis=0)


def _top16(s):
    n = s.shape[0]
    rows = lax.broadcasted_iota(jnp.int32, s.shape, 0)
    row16 = lax.broadcasted_iota(jnp.int32, (PEER_TOPK, s.shape[1]), 0)
    rank = jnp.full(s.shape, RANK_NONE, jnp.float32)
    sv = jnp.zeros((PEER_TOPK, s.shape[1]), jnp.float32)
    v = s
    for k in range(PEER_TOPK):
        m = jnp.max(v, axis=0, keepdims=True)
        idx = jnp.min(jnp.where(v == m, rows, n), axis=0, keepdims=True)
        hit = rows == idx
        rank = jnp.where(hit, float(k), rank)
        v = jnp.where(hit, -jnp.inf, v)
        sv = jnp.where(row16 == k, m, sv)
    return rank, sv


_CAND_GROUPS = ((0, 0, 8), (0, 8, 8), (1, 0, 8), (2, 0, 5), (3, 0, 4), (4, 0, 3), (5, 0, 2), (6, 0, 2), (7, 0, 2))


def _select_pairs(sv0, sv1):
    lanes = sv0.shape[1]
    sub = lax.broadcasted_iota(jnp.int32, (V7X_SUBLANES, lanes), 0)
    cands, poss = [], []
    for r0, r1_0, cnt in _CAND_GROUPS:
        c = sv0[r0:r0 + 1] + sv1[r1_0:r1_0 + V7X_SUBLANES]
        cands.append(jnp.where(sub < cnt, c, -jnp.inf))
        poss.append(r0 * PEER_TOPK + r1_0 + sub)
    cands.append(sv0[V7X_SUBLANES:] + sv1[0:1])
    poss.append((V7X_SUBLANES + sub) * PEER_TOPK)
    cand = jnp.concatenate(cands, axis=0)
    pos = jnp.concatenate(poss, axis=0)
    cmax = sv0[0:1] + sv1[0:1]
    big = PEER_TOPK * PEER_TOPK
    c = cand
    sel = jnp.zeros(cand.shape, jnp.float32)
    for _ in range(PEER_TOPK):
        m = jnp.max(c, axis=0, keepdims=True)
        pidx = jnp.min(jnp.where(c == m, pos, big), axis=0, keepdims=True)
        hit = pos == pidx
        sel = jnp.where(hit, 1.0, sel)
        c = jnp.where(hit, -jnp.inf, c)
    z = jnp.sum(jnp.where(sel > 0.0, jnp.exp(cand - cmax), 0.0), axis=0, keepdims=True)
    g = V7X_SUBLANES
    rows = [jnp.sum(sel[0:2 * g], axis=0, keepdims=True)]
    for i in range(2, len(_CAND_GROUPS)):
        rows.append(jnp.sum(sel[i * g:(i + 1) * g], axis=0, keepdims=True))
    rows.append(sel[len(_CAND_GROUPS) * g:])
    return jnp.concatenate(rows, axis=0), z


def _peer_select_kernel(q_ref, keys_ref, r1_ref, b_ref, cnt_ref, a_ref):
    def head(hh, carry):
        base = pl.multiple_of(hh * 2 * PEER_HALF, 2 * PEER_HALF)
        q0 = q_ref[pl.ds(base, PEER_HALF), :]
        q1 = q_ref[pl.ds(base + PEER_HALF, PEER_HALF), :]
        s0 = jnp.dot(keys_ref[hh, 0], q0, preferred_element_type=jnp.float32, precision=lax.Precision.HIGHEST)
        s1 = jnp.dot(keys_ref[hh, 1], q1, preferred_element_type=jnp.float32, precision=lax.Precision.HIGHEST)
        rank0, sv0 = _top16(s0)
        rank1, sv1 = _top16(s1)
        cnt16, z = _select_pairs(sv0, sv1)
        cnt = jnp.zeros(s0.shape, jnp.float32)
        for r in range(PEER_TOPK):
            cnt = jnp.where(rank0 == float(r), cnt16[r:r + 1], cnt)
        r1_ref[hh] = rank1
        b_ref[hh] = jnp.exp(s1 - sv1[0:1])
        cnt_ref[hh] = cnt
        a_ref[hh] = jnp.exp(s0 - sv0[0:1]) / z
        return carry

    lax.fori_loop(0, PEER_HEADS, head, 0, unroll=4)


def _peer_select(qt, sub_keys):
    _, t = qt.shape
    tl = _pick(t, (2 * V7X_LANES, V7X_LANES))
    shp = jax.ShapeDtypeStruct((PEER_HEADS, PEER_N_KEYS, t), jnp.float32)
    ospec = pl.BlockSpec((PEER_HEADS, PEER_N_KEYS, tl), lambda i: (0, 0, i))
    return pl.pallas_call(
        _peer_select_kernel,
        grid=(t // tl,),
        in_specs=[pl.BlockSpec((qt.shape[0], tl), lambda i: (0, i)),
                  pl.BlockSpec(sub_keys.shape, lambda i: (0, 0, 0, 0))],
        out_specs=[ospec] * 4,
        out_shape=[shp] * 4,
        compiler_params=_params("parallel"),
        name="peer_select",
    )(qt, sub_keys)


def _gelu_exact(x):
    return 0.5 * x * (1.0 + lax.erf(x * (2.0 ** -0.5)))


def _peer_act_kernel(u_ref, h_ref, r1_ref, b_ref, cnt_ref, a_ref, o_ref, *, iblocks):
    nk = PEER_N_KEYS
    groups = [slice(g * MXU_COLS, (g + 1) * MXU_COLS) for g in range(h_ref.shape[1] // MXU_COLS)]
    pres = [jnp.dot(u_ref[...], h_ref[:, cols], preferred_element_type=jnp.float32) for cols in groups]
    dt = o_ref.dtype
    for pre, cols in zip(pres, groups):
        r1 = [r1_ref[hh, :, cols].astype(dt) for hh in range(PEER_HEADS)]
        b = [b_ref[hh, :, cols].astype(dt) for hh in range(PEER_HEADS)]
        for ii in range(iblocks):
            w = jnp.zeros((nk, MXU_COLS), dt)
            for hh in range(PEER_HEADS):
                sel = r1[hh] < cnt_ref[hh, ii:ii + 1, cols].astype(dt)
                w = w + jnp.where(sel, b[hh], jnp.zeros_like(b[hh])) * a_ref[hh, ii:ii + 1, cols].astype(dt)
            o_ref[ii * nk:(ii + 1) * nk, cols] = _gelu_exact(pre[ii * nk:(ii + 1) * nk]).astype(dt) * w


def _peer_act(u, layer, ht, r1, b, cnt, a):
    _, e, d = u.shape
    _, t = ht.shape
    iblocks = V7X_SUBLANES
    te = iblocks * PEER_N_KEYS
    tt = _pick(t, TOKEN_TILES)
    full = pl.BlockSpec((PEER_HEADS, PEER_N_KEYS, tt), lambda i, j: (0, 0, i))
    part = pl.BlockSpec((PEER_HEADS, iblocks, tt), lambda i, j: (0, j, i))
    return pl.pallas_call(
        functools.partial(_peer_act_kernel, iblocks=iblocks),
        grid=(t // tt, e // te),
        in_specs=[pl.BlockSpec((None, te, d), lambda i, j: (layer, j, 0)),
                  pl.BlockSpec((d, tt), lambda i, j: (0, i)),
                  full, full, part, part],
        out_specs=pl.BlockSpec((te, tt), lambda i, j: (j, i)),
        out_shape=jax.ShapeDtypeStruct((e, t), MXU_DTYPE),
        compiler_params=_params("parallel", "parallel"),
        name="peer_act",
    )(u, ht, r1, b, cnt, a)


def _peer(x, nw, mod_ctx, mod_lat, gates, tc, w_q, sub_keys, u_all, vt_all, layer):
    (ht,) = _modulate(x, nw, mod_ctx, mod_lat, tc, ("feature_major",))
    qt = _mm(w_q.T.astype(MXU_DTYPE), ht, jnp.float32, tm_prefs=(512, 256, 128),
             tn_prefs=TOKEN_TILES, name="peer_query")
    r1, b, cnt, a = _peer_select(qt, sub_keys)
    act = _peer_act(u_all, layer, ht, r1, b, cnt, a)
    return _mm_res_t(vt_all, act, x, gates, tc, layer=layer, name="peer_out")


def kernel(x, c, ctx, c_ctx, ada_w, ada_b, mix_norm, ffn_norm, a_w_qkv, a_w_o, a_q_norm, a_k_norm,
           b_w_qkv, b_w_o, b_q_norm, b_k_norm, b_sink, f_w_out, peer_w_q, peer_sub_keys, peer_u, peer_v):
    batch, seq, d = x.shape
    assert batch == 1 and ctx.shape[0] == 1
    tc = ctx.shape[1]
    depth = ada_w.shape[0]
    pad = -(tc + seq) % TOKEN_TILES[0]
    xs = jnp.concatenate([ctx[0], x[0], jnp.zeros((pad, d), x.dtype)], axis=0)
    mods = _ada(jnp.stack([c[0], c_ctx]), ada_w, ada_b)
    mods = mods.reshape(depth, 2, N_MOD, d)
    u_all = peer_u.astype(MXU_DTYPE)
    vt_all = peer_v.transpose(0, 2, 1).astype(MXU_DTYPE)
    for layer in range(depth):
        last = layer == depth - 1
        kind, j = layer % N_MIXERS, layer // N_MIXERS
        lat, cx = mods[layer, 0], mods[layer, 1]
        gates = jnp.stack([cx[2], lat[2]])
        if kind == 2:
            (h,) = _modulate(xs, mix_norm[layer], cx[0:2], lat[0:2], tc, ("token_major",))
            xs = _mm_res(_fourier_mix(h, tc, seq), f_w_out[j].astype(MXU_DTYPE), xs, gates, tc, name="mixer_out")
        else:
            (ht,) = _modulate(xs, mix_norm[layer], cx[0:2], lat[0:2], tc, ("feature_major",))
            if kind == 0:
                xs = _attention_layer(xs, ht, gates, a_w_qkv[j], a_w_o[j], a_q_norm[j], a_k_norm[j], None,
                                      A_HEAD_DIM, A_GROUP, tc, seq, False, not last)
            else:
                xs = _attention_layer(xs, ht, gates, b_w_qkv[j], b_w_o[j], b_q_norm[j], b_k_norm[j], b_sink[j],
                                      B_HEAD_DIM, B_GROUP, tc, seq, True, not last)
        xs = _peer(xs, ffn_norm[layer], cx[3:5], lat[3:5], jnp.stack([cx[5], lat[5]]), tc,
                   peer_w_q[layer], peer_sub_keys[layer], u_all, vt_all, layer)
    return xs[tc:tc + seq][None]
```
